```python
import math
import jax, jax.numpy as jnp
from jax import lax
import numpy as np

D_MODEL = 1024
BATCH = 2
SEQ = 8192
DEPTH = 2
DEC_BATCH = 32
DEC_SEQ = 1
PAST_LEN = 16384
PAGE_SIZE = 128

HEAD_DIM = 64
D_MIX = D_MODEL
A_HEADS = D_MODEL // 256
A_QK_HEADS = 2 * A_HEADS
A_WIDTH = A_HEADS * 2 * HEAD_DIM
G_WIDTH = D_MODEL // 4
G_HEADS = G_WIDTH // HEAD_DIM
CHUNK = 128
P_WIDTH = D_MIX - A_WIDTH - G_WIDTH
POOL_WINDOWS = (2, 4, 8, 16)
P_GROUPS = len(POOL_WINDOWS)
P_GC = P_WIDTH // P_GROUPS
POOL_HIST = max(POOL_WINDOWS) - 1
PROJ_SIZES = (A_QK_HEADS * HEAD_DIM, A_QK_HEADS * HEAD_DIM, A_WIDTH, G_WIDTH, G_WIDTH, P_WIDTH)
D_IN = sum(PROJ_SIZES)
ROPE_THETA = 10000.0
Q_BLOCK = 128
D_FF = ((8 * D_MODEL // 3 + 127) // 128) * 128
N_EXPERTS = 8
TOP_K = 2
D_FF_E = 7 * D_MODEL // 2
N_DENSE = (DEPTH + 1) // 2
N_MOE = DEPTH // 2
EPS = 1e-6

kernel_name = 'hybrid_diffattn_gmlp_pool_decode_step'


def rms_norm(x, g):
    xf = x.astype(jnp.float32)
    y = xf * lax.rsqrt(jnp.mean(xf * xf, axis=-1, keepdims=True) + EPS)
    return (y * g.astype(jnp.float32)).astype(x.dtype)


def layer_norm(x, g, b):
    xf = x.astype(jnp.float32)
    mu = jnp.mean(xf, axis=-1, keepdims=True)
    xc = xf - mu
    y = xc * lax.rsqrt(jnp.mean(xc * xc, axis=-1, keepdims=True) + EPS)
    return (y * g.astype(jnp.float32) + b.astype(jnp.float32)).astype(x.dtype)


def rope(x, pos):
    half = HEAD_DIM // 2
    inv = ROPE_THETA ** (-jnp.arange(half, dtype=jnp.float32) / half)
    ang = pos.astype(jnp.float32)[:, None] * inv[None, :]
    cos = jnp.cos(ang)[None, :, None, :]
    sin = jnp.sin(ang)[None, :, None, :]
    xf = x.astype(jnp.float32)
    x1, x2 = xf[..., :half], xf[..., half:]
    return jnp.concatenate([x1 * cos - x2 * sin, x2 * cos + x1 * sin], axis=-1).astype(x.dtype)


def split_proj(z):
    offs = [int(o) for o in np.cumsum(PROJ_SIZES)[:-1]]
    return jnp.split(z, offs, axis=-1)


def branch_inputs(h, pos, w_in_l, q_g, k_g, gn_g, gn_b):
    b, s = h.shape[:2]
    q, k, v, gu, gv, px = split_proj(h @ w_in_l)
    q = rope(rms_norm(q.reshape(b, s, A_QK_HEADS, HEAD_DIM), q_g), pos)
    k = rope(rms_norm(k.reshape(b, s, A_QK_HEADS, HEAD_DIM), k_g), pos)
    v = v.reshape(b, s, A_HEADS, 2 * HEAD_DIM)
    gu = gu.reshape(b, s, G_HEADS, HEAD_DIM)
    gvn = layer_norm(gv, gn_g, gn_b).reshape(b, s, G_HEADS, HEAD_DIM)
    return q, k, v, gu, gvn, px


def diff_weights(p, lam, b, nq, nk):
    p = p.reshape(b, A_HEADS, 2, nq, nk)
    return p[:, :, 0] - lam * p[:, :, 1]


def diff_attn_prompt(q, k, v, lam):
    b, s = q.shape[:2]
    nb = s // Q_BLOCK
    scale = HEAD_DIM ** -0.5
    qb = q.reshape(b, nb, Q_BLOCK, A_QK_HEADS, HEAD_DIM).transpose(1, 0, 2, 3, 4)
    k_pos = jnp.arange(s)

    def one_block(args):
        qi, i = args
        q_pos = i * Q_BLOCK + jnp.arange(Q_BLOCK)
        sc = jnp.einsum('bqhd,bkhd->bhqk', qi, k, preferred_element_type=jnp.float32) * scale
        sc = jnp.where(k_pos[None, :] <= q_pos[:, None], sc, -jnp.inf)
        w = diff_weights(jax.nn.softmax(sc, axis=-1), lam, b, Q_BLOCK, s).astype(v.dtype)
        return jnp.einsum('bhqk,bkhe->bqhe', w, v)

    out = lax.map(one_block, (qb, jnp.arange(nb)))
    return out.transpose(1, 0, 2, 3, 4).reshape(b, s, A_HEADS, 2 * HEAD_DIM)


def diff_attn_sample(q, k_new, v_new, k_past, v_past, lam):
    b, ds = q.shape[:2]
    npast = k_past.shape[1]
    scale = HEAD_DIM ** -0.5
    sc_past = jnp.einsum('bqhd,bkhd->bhqk', q, k_past, preferred_element_type=jnp.float32) * scale
    sc_new = jnp.einsum('bqhd,bkhd->bhqk', q, k_new, preferred_element_type=jnp.float32) * scale
    causal = jnp.tril(jnp.ones((ds, ds), dtype=bool))
    sc_new = jnp.where(causal, sc_new, -jnp.inf)
    p = jax.nn.softmax(jnp.concatenate([sc_past, sc_new], axis=-1), axis=-1)
    w = diff_weights(p, lam, b, ds, npast + ds).astype(v_new.dtype)
    return (jnp.einsum('bhqk,bkhe->bqhe', w[..., :npast], v_past)
            + jnp.einsum('bhqk,bkhe->bqhe', w[..., npast:], v_new))


def gmlp_prompt(u, vn, ws, bs):
    b, s = u.shape[:2]
    w = ws * jnp.tril(jnp.ones((CHUNK, CHUNK), dtype=ws.dtype))[None]
    vc = vn.reshape(b, s // CHUNK, CHUNK, G_HEADS, HEAD_DIM)
    mixed = jnp.einsum('gts,bcsgd->bctgd', w, vc) + bs.T[None, None, :, :, None]
    return u * mixed.reshape(b, s, G_HEADS, HEAD_DIM)


def gmlp_sample(u, vn, ws, bs):
    ds = u.shape[1]
    r = jnp.arange(ds)
    cp = r % CHUNK
    mask = ((r[:, None] // CHUNK) == (r[None, :] // CHUNK)) & (r[None, :] <= r[:, None])
    w = jnp.where(mask[None], ws[:, cp[:, None], cp[None, :]], 0.0).astype(ws.dtype)
    mixed = jnp.einsum('gts,bsgd->btgd', w, vn) + bs[:, cp].T[None, :, :, None]
    return u * mixed


def pool_mix(x_hist, x_cur, start_pos, pool_w_l, pool_scale_l):
    b, s = x_cur.shape[:2]
    n_hist = x_hist.shape[1]
    ext = jnp.concatenate([x_hist, x_cur], axis=1).astype(jnp.float32)
    cs = jnp.concatenate([jnp.zeros((b, 1, P_WIDTH), jnp.float32), jnp.cumsum(ext, axis=1)], axis=1)
    idx = n_hist + jnp.arange(s)
    abs_pos = start_pos + jnp.arange(s)
    means = []
    for g, win in enumerate(POOL_WINDOWS):
        lo = jnp.maximum(idx + 1 - win, 0)
        csg = cs[:, :, g * P_GC:(g + 1) * P_GC]
        tot = csg[:, idx + 1] - csg[:, lo]
        cnt = jnp.minimum(win, abs_pos + 1).astype(jnp.float32)
        means.append(tot / cnt[None, :, None])
    d = (jnp.concatenate(means, axis=-1) - x_cur.astype(jnp.float32)).astype(x_cur.dtype)
    y = jnp.einsum('bsgc,gce->bsge', d.reshape(b, s, P_GROUPS, P_GC), pool_w_l).reshape(b, s, P_WIDTH)
    return y * pool_scale_l


def merge_out(a, g, p, subln_g_l, lam_init, w_out_l):
    b, s = a.shape[:2]
    a = (rms_norm(a, subln_g_l) * (1.0 - lam_init)).reshape(b, s, A_WIDTH)
    return jnp.concatenate([a, g.reshape(b, s, G_WIDTH), p], axis=-1) @ w_out_l


def swiglu(h, wg, wu, wd):
    return (jax.nn.silu(h @ wg) * (h @ wu)) @ wd


def moe_ffn(h, router_w_l, wg, wu, wd):
    shp = h.shape
    t = h.reshape(-1, D_MODEL)
    logits = (t @ router_w_l).astype(jnp.float32)
    top_v, top_i = lax.top_k(logits, TOP_K)
    gates = jax.nn.softmax(top_v, axis=-1)
    combine = jnp.sum(jax.nn.one_hot(top_i, N_EXPERTS, dtype=jnp.float32) * gates[..., None], axis=1)
    y = jnp.zeros_like(t)
    for e in range(N_EXPERTS):
        y = y + combine[:, e:e + 1].astype(t.dtype) * swiglu(t, wg[e], wu[e], wd[e])
    return y.reshape(shp)


def setup_inputs(seed: int = 0) -> dict:
    key = jax.random.key(seed)
    ks = jax.random.split(key, 40)
    f32 = jnp.float32

    def nrm(k, shape, scale):
        return jax.random.normal(k, shape, f32) * scale

    def gain(k, shape):
        return 1.0 + 0.02 * jax.random.normal(k, shape, f32)

    n_pages = PAST_LEN // PAGE_SIZE
    n_used = DEC_BATCH * n_pages
    n_pool = (n_used * 5) // 4
    perm = jax.random.permutation(ks[0], n_pool)
    page_table = perm[:n_used].reshape(DEC_BATCH, n_pages).astype(jnp.int32)
    return {
        'x_prompt': nrm(ks[1], (BATCH, SEQ, D_MODEL), 1.0),
        'x_sample': nrm(ks[2], (DEC_BATCH, DEC_SEQ, D_MODEL), 1.0),
        'cache_k': nrm(ks[3], (DEPTH, n_pool, PAGE_SIZE, A_QK_HEADS, HEAD_DIM), 1.0),
        'cache_v': nrm(ks[4], (DEPTH, n_pool, PAGE_SIZE, A_HEADS, 2 * HEAD_DIM), 1.0),
        'state_pool': nrm(ks[5], (DEPTH, DEC_BATCH, POOL_HIST, P_WIDTH), 1.0),
        'page_table': page_table,
        'norm_mix_g': gain(ks[6], (DEPTH, D_MODEL)),
        'w_in': nrm(ks[7], (DEPTH, D_MODEL, D_IN), D_MODEL ** -0.5),
        'q_norm_g': gain(ks[8], (DEPTH, HEAD_DIM)),
        'k_norm_g': gain(ks[9], (DEPTH, HEAD_DIM)),
        'lam_q1': nrm(ks[10], (DEPTH, HEAD_DIM), 0.1),
        'lam_k1': nrm(ks[11], (DEPTH, HEAD_DIM), 0.1),
        'lam_q2': nrm(ks[12], (DEPTH, HEAD_DIM), 0.1),
        'lam_k2': nrm(ks[13], (DEPTH, HEAD_DIM), 0.1),
        'subln_g': gain(ks[14], (DEPTH, 2 * HEAD_DIM)),
        'gmlp_norm_g': gain(ks[15], (DEPTH, G_WIDTH)),
        'gmlp_norm_b': nrm(ks[16], (DEPTH, G_WIDTH), 0.02),
        'gmlp_ws': nrm(ks[17], (DEPTH, G_HEADS, CHUNK, CHUNK), CHUNK ** -0.5),
        'gmlp_bs': gain(ks[18], (DEPTH, G_HEADS, CHUNK)),
        'pool_w': nrm(ks[19], (DEPTH, P_GROUPS, P_GC, P_GC), P_GC ** -0.5),
        'pool_scale': gain(ks[20], (DEPTH, P_WIDTH)),
        'w_out': nrm(ks[21], (DEPTH, D_MIX, D_MODEL), D_MIX ** -0.5),
        'norm_ffn_g': gain(ks[22], (DEPTH, D_MODEL)),
        'ffn_w_gate': nrm(ks[23], (N_DENSE, D_MODEL, D_FF), D_MODEL ** -0.5),
        'ffn_w_up': nrm(ks[24], (N_DENSE, D_MODEL, D_FF), D_MODEL ** -0.5),
        'ffn_w_down': nrm(ks[25], (N_DENSE, D_FF, D_MODEL), D_FF ** -0.5),
        'router_w': nrm(ks[26], (N_MOE, D_MODEL, N_EXPERTS), D_MODEL ** -0.5),
        'moe_w_gate': nrm(ks[27], (N_MOE, N_EXPERTS, D_MODEL, D_FF_E), D_MODEL ** -0.5),
        'moe_w_up': nrm(ks[28], (N_MOE, N_EXPERTS, D_MODEL, D_FF_E), D_MODEL ** -0.5),
        'moe_w_down': nrm(ks[29], (N_MOE, N_EXPERTS, D_FF_E, D_MODEL), D_FF_E ** -0.5),
    }


def reference(x_prompt, x_sample, cache_k, cache_v, state_pool, page_table, norm_mix_g, w_in,
              q_norm_g, k_norm_g, lam_q1, lam_k1, lam_q2, lam_k2, subln_g, gmlp_norm_g, gmlp_norm_b,
              gmlp_ws, gmlp_bs, pool_w, pool_scale, w_out, norm_ffn_g, ffn_w_gate, ffn_w_up,
              ffn_w_down, router_w, moe_w_gate, moe_w_up, moe_w_down):
    pos_p = jnp.arange(SEQ)
    pos_s = PAST_LEN + jnp.arange(DEC_SEQ)
    xp, xs = x_prompt, x_sample
    kp_l, vp_l, pp_l, ks_l, vs_l, ps_l, gs_l = [], [], [], [], [], [], []
    for l in range(DEPTH):
        lam_init = 0.8 - 0.6 * math.exp(-0.3 * l)
        lam = (jnp.exp(jnp.sum(lam_q1[l].astype(jnp.float32) * lam_k1[l].astype(jnp.float32)))
               - jnp.exp(jnp.sum(lam_q2[l].astype(jnp.float32) * lam_k2[l].astype(jnp.float32)))
               + lam_init)

        hp = rms_norm(xp, norm_mix_g[l])
        q, k, v, gu, gvn, px = branch_inputs(hp, pos_p, w_in[l], q_norm_g[l], k_norm_g[l],
                                             gmlp_norm_g[l], gmlp_norm_b[l])
        a = diff_attn_prompt(q, k, v, lam)
        g = gmlp_prompt(gu, gvn, gmlp_ws[l], gmlp_bs[l])
        p = pool_mix(px[:, :0], px, 0, pool_w[l], pool_scale[l])
        xp = xp + merge_out(a, g, p, subln_g[l], lam_init, w_out[l])
        kp_l.append(k)
        vp_l.append(v)
        pp_l.append(px[:, SEQ - POOL_HIST:])

        hs = rms_norm(xs, norm_mix_g[l])
        q, k, v, gu, gvn, px = branch_inputs(hs, pos_s, w_in[l], q_norm_g[l], k_norm_g[l],
                                             gmlp_norm_g[l], gmlp_norm_b[l])
        k_past = cache_k[l, page_table].reshape(DEC_BATCH, -1, A_QK_HEADS, HEAD_DIM)
        v_past = cache_v[l, page_table].reshape(DEC_BATCH, -1, A_HEADS, 2 * HEAD_DIM)
        a = diff_attn_sample(q, k, v, k_past, v_past, lam)
        g = gmlp_sample(gu, gvn, gmlp_ws[l], gmlp_bs[l])
        p = pool_mix(state_pool[l], px, PAST_LEN, pool_w[l], pool_scale[l])
        xs = xs + merge_out(a, g, p, subln_g[l], lam_init, w_out[l])
        ks_l.append(k)
        vs_l.append(v)
        ps_l.append(jnp.concatenate([state_pool[l], px], axis=1)[:, -POOL_HIST:])
        gs_l.append(gvn)

        hp = rms_norm(xp, norm_ffn_g[l])
        hs = rms_norm(xs, norm_ffn_g[l])
        if l % 2 == 0:
            j = l // 2
            xp = xp + swiglu(hp, ffn_w_gate[j], ffn_w_up[j], ffn_w_down[j])
            xs = xs + swiglu(hs, ffn_w_gate[j], ffn_w_up[j], ffn_w_down[j])
        else:
            j = l // 2
            xp = xp + moe_ffn(hp, router_w[j], moe_w_gate[j], moe_w_up[j], moe_w_down[j])
            xs = xs + moe_ffn(hs, router_w[j], moe_w_gate[j], moe_w_up[j], moe_w_down[j])

    new_k_prompt = jnp.stack(kp_l)
    new_v_prompt = jnp.stack(vp_l)
    new_pool_prompt = jnp.stack(pp_l)
    new_k_sample = jnp.stack(ks_l)
    new_v_sample = jnp.stack(vs_l)
    new_pool_sample = jnp.stack(ps_l)
    new_gmlp_v_sample = jnp.stack(gs_l)
    return (xp, xs, new_k_prompt, new_v_prompt, new_pool_prompt, new_k_sample, new_v_sample, new_pool_sample, new_gmlp_v_sample)
```

```python
import functools
import math

import jax
import jax.numpy as jnp
from jax import lax
from jax.experimental import pallas as pl
from jax.experimental.pallas import tpu as pltpu

F32 = jnp.float32
BF16 = jnp.bfloat16

D_MODEL = 1024
HEAD_DIM = 64
A_HEADS = 4
A_QK_HEADS = 8
A_WIDTH = 512
G_WIDTH = 256
G_HEADS = 4
CHUNK = 128
P_WIDTH = 256
POOL_WINDOWS = (2, 4, 8, 16)
P_GC = 64
POOL_HIST = 15
D_IN = 2304
ROPE_THETA = 10000.0
N_EXPERTS = 8
TOP_K = 2
EPS = 1e-6
PAGE_SIZE = 128

LANES = 128
VMEM_LIMIT = 56 * 1024 * 1024

TM_PROMPT = 512
TQ = 1024
TM_MOE = 512
TF_MOE = 896
PAGES_PER_STEP = 8


def _cparams(sem):
    return pltpu.CompilerParams(dimension_semantics=sem, vmem_limit_bytes=VMEM_LIMIT)


def _lam_init(layer):
    return 0.8 - 0.6 * math.exp(-0.3 * layer)


def _lam_value(lq1, lk1, lq2, lk2, lam_init):
    a = jnp.sum(lq1 * lk1, axis=-1, keepdims=True)
    b = jnp.sum(lq2 * lk2, axis=-1, keepdims=True)
    return jnp.exp(a) - jnp.exp(b) + lam_init


def _swap_halves(x):
    lane = lax.broadcasted_iota(jnp.int32, x.shape, 1)
    first_half = (lane % HEAD_DIM) < (HEAD_DIM // 2)
    return jnp.where(first_half, pltpu.roll(x, LANES - HEAD_DIM // 2, 1),
                     pltpu.roll(x, HEAD_DIM // 2, 1))


def _head_norm_rope(t, head_ones, gain, cos, sin):
    sq = t * t
    ss = jnp.dot(sq.astype(BF16), head_ones, preferred_element_type=F32)
    ss = ss + jnp.dot((sq - sq.astype(BF16).astype(F32)).astype(BF16), head_ones,
                      preferred_element_type=F32)
    tn = t * lax.rsqrt(ss * (1.0 / HEAD_DIM) + EPS) * gain
    cols = []
    for c in range(t.shape[1] // LANES):
        tc = tn[:, c * LANES:(c + 1) * LANES]
        cols.append(tc * cos + _swap_halves(tc) * sin)
    return jnp.concatenate(cols, axis=1)


def _project(x, g_mix, w_in, head_ones, qg, kg, cos, sin, gn_g, gn_b):
    ms = jnp.mean(x * x, axis=-1, keepdims=True)
    h = x * lax.rsqrt(ms + EPS) * g_mix
    z = jnp.dot(h.astype(BF16), w_in, preferred_element_type=F32)
    q = _head_norm_rope(z[:, 0:512], head_ones, qg, cos, sin)
    k = _head_norm_rope(z[:, 512:1024], head_ones, kg, cos, sin)
    v = z[:, 1024:1536]
    gu = z[:, 1536:1792]
    gv = z[:, 1792:2048]
    px = z[:, 2048:2304]
    mu = jnp.mean(gv, axis=-1, keepdims=True)
    gc = gv - mu
    var = jnp.mean(gc * gc, axis=-1, keepdims=True)
    gvn = gc * lax.rsqrt(var + EPS) * gn_g + gn_b
    return q, k, v, gu, gvn, px


def _pool_window_select(s2, s4, s8, s16):
    lane = lax.broadcasted_iota(jnp.int32, s2.shape, 1)
    return jnp.where(lane < P_GC, s2,
                     jnp.where(lane < 2 * P_GC, s4, jnp.where(lane < 3 * P_GC, s8, s16)))


def _pool_window_sizes(shape):
    lane = lax.broadcasted_iota(jnp.int32, shape, 1)
    return jnp.where(lane < P_GC, 2.0,
                     jnp.where(lane < 2 * P_GC, 4.0,
                               jnp.where(lane < 3 * P_GC, 8.0, 16.0))).astype(F32)


def _mix_in_prompt_kernel(x_ref, gmix_ref, win_ref, ones_ref, qg_ref, kg_ref, cos_ref, sin_ref,
                          gng_ref, gnb_ref, gw_ref, gb_ref, pw_ref, ps_ref,
                          q_ref, kt_ref, v_ref, gp_ref, pxl_ref, ext_ref, *, tm, tiles_per_seq):
    i = pl.program_id(0)
    q, k, v, gu, gvn, px = _project(
        x_ref[...], gmix_ref[...], win_ref[...], ones_ref[...], qg_ref[...], kg_ref[...],
        cos_ref[...], sin_ref[...], gng_ref[...], gnb_ref[...])
    q_ref[...] = (q * (HEAD_DIM ** -0.5)).astype(BF16)
    kt_ref[0] = k.T
    v_ref[...] = v
    pxl_ref[...] = px

    lane = lax.broadcasted_iota(jnp.int32, (CHUNK, G_WIDTH), 1)
    for c in range(tm // CHUNK):
        vn_c = gvn[c * CHUNK:(c + 1) * CHUNK, :]
        rhs = jnp.concatenate(
            [jnp.where((lane // HEAD_DIM) == g, vn_c, 0.0).astype(BF16) for g in range(G_HEADS)],
            axis=0)
        mixed = jnp.dot(gw_ref[...], rhs, preferred_element_type=F32) + gb_ref[...]
        gp_ref[c * CHUNK:(c + 1) * CHUNK, 0:G_WIDTH] = (
            gu[c * CHUNK:(c + 1) * CHUNK, :] * mixed).astype(BF16)

    seq_tile = i % tiles_per_seq

    @pl.when(seq_tile == 0)
    def _():
        ext_ref[0:16, :] = jnp.zeros((16, P_WIDTH), F32)

    @pl.when(seq_tile != 0)
    def _():
        ext_ref[0:16, :] = ext_ref[tm:tm + 16, :]

    ext_ref[16:16 + tm, :] = px

    def back(kk):
        return ext_ref[16 - kk:16 - kk + tm, :]

    s2 = px + back(1)
    s4 = s2 + back(2) + back(3)
    s8 = s4 + back(4) + back(5) + back(6) + back(7)
    s16 = s8
    for kk in range(8, 16):
        s16 = s16 + back(kk)
    tot = _pool_window_select(s2, s4, s8, s16)
    pos = (seq_tile * tm + lax.broadcasted_iota(jnp.int32, (tm, P_WIDTH), 0)).astype(F32)
    cnt = jnp.minimum(_pool_window_sizes((tm, P_WIDTH)), pos + 1.0)
    d = tot / cnt - px
    p = jnp.dot(d.astype(BF16), pw_ref[...], preferred_element_type=F32) * ps_ref[...]
    gp_ref[:, G_WIDTH:G_WIDTH + P_WIDTH] = p.astype(BF16)


def _mix_in_sample_kernel(x_ref, gmix_ref, win_ref, ones_ref, qg_ref, kg_ref, cos_ref, sin_ref,
                          gng_ref, gnb_ref, gw0_ref, gb0_ref, pw_ref, ps_ref, hist_ref,
                          q_ref, k_ref, v_ref, gp_ref, px_ref, gvn_ref):
    q, k, v, gu, gvn, px = _project(
        x_ref[...], gmix_ref[...], win_ref[...], ones_ref[...], qg_ref[...], kg_ref[...],
        cos_ref[...], sin_ref[...], gng_ref[...], gnb_ref[...])
    q_ref[...] = q * (HEAD_DIM ** -0.5)
    k_ref[...] = k
    v_ref[...] = v
    px_ref[...] = px
    gvn_ref[...] = gvn
    mixed = gw0_ref[...] * gvn.astype(BF16).astype(F32) + gb0_ref[...]
    gp_ref[:, 0:G_WIDTH] = (gu * mixed).astype(BF16)

    def tail(n):
        acc = px
        for r in range(POOL_HIST - n, POOL_HIST):
            acc = acc + hist_ref[r]
        return acc

    tot = _pool_window_select(tail(1), tail(3), tail(7), tail(15))
    d = tot / _pool_window_sizes(px.shape) - px
    p = jnp.dot(d.astype(BF16), pw_ref[...], preferred_element_type=F32) * ps_ref[...]
    gp_ref[:, G_WIDTH:G_WIDTH + P_WIDTH] = p.astype(BF16)


def _const_spec(shape):
    return pl.BlockSpec(shape, lambda *_: (0,) * len(shape))


def _mix_in_prompt(x, prm, cos, sin, seq_len):
    t = x.shape[0]
    tm = TM_PROMPT
    tps = seq_len // tm
    nseq = t // seq_len
    kern = functools.partial(_mix_in_prompt_kernel, tm=tm, tiles_per_seq=tps)
    row = lambda w: pl.BlockSpec((tm, w), lambda i: (i, 0))
    return pl.pallas_call(
        kern,
        grid=(t // tm,),
        in_specs=[
            row(D_MODEL), _const_spec((1, D_MODEL)), _const_spec((D_MODEL, D_IN)),
            _const_spec((512, 512)), _const_spec((1, 512)), _const_spec((1, 512)),
            pl.BlockSpec((tm, LANES), lambda i: (i % tps, 0)),
            pl.BlockSpec((tm, LANES), lambda i: (i % tps, 0)),
            _const_spec((1, G_WIDTH)), _const_spec((1, G_WIDTH)),
            _const_spec((CHUNK, G_HEADS * CHUNK)), _const_spec((CHUNK, G_WIDTH)),
            _const_spec((P_WIDTH, P_WIDTH)), _const_spec((1, P_WIDTH)),
        ],
        out_specs=[
            row(512),
            pl.BlockSpec((1, 512, tm), lambda i: (i // tps, 0, i % tps)),
            row(512), row(512),
            pl.BlockSpec((tm, P_WIDTH), lambda i: (i // tps, 0)),
        ],
        out_shape=[
            jax.ShapeDtypeStruct((t, 512), BF16),
            jax.ShapeDtypeStruct((nseq, 512, seq_len), F32),
            jax.ShapeDtypeStruct((t, 512), F32),
            jax.ShapeDtypeStruct((t, 512), BF16),
            jax.ShapeDtypeStruct((nseq * tm, P_WIDTH), F32),
        ],
        scratch_shapes=[pltpu.VMEM((tm + 16, P_WIDTH), F32)],
        compiler_params=_cparams(("arbitrary",)),
        name="mix_in_prompt",
    )(x, prm["g_mix"], prm["w_in"], prm["head_ones"], prm["qg"], prm["kg"], cos, sin,
      prm["gn_g"], prm["gn_b"], prm["gw_cat"], prm["gb_full"], prm["pool_bd"], prm["pool_scale"])


def _mix_in_sample(x, prm, cos, sin, hist):
    n = x.shape[0]
    full = lambda *s: _const_spec(s)
    return pl.pallas_call(
        _mix_in_sample_kernel,
        grid=(1,),
        in_specs=[
            full(n, D_MODEL), full(1, D_MODEL), full(D_MODEL, D_IN), full(512, 512),
            full(1, 512), full(1, 512), full(n, LANES), full(n, LANES),
            full(1, G_WIDTH), full(1, G_WIDTH), full(1, G_WIDTH), full(1, G_WIDTH),
            full(P_WIDTH, P_WIDTH), full(1, P_WIDTH), full(POOL_HIST, n, P_WIDTH),
        ],
        out_specs=[full(n, 512), full(n, 512), full(n, 512), full(n, 512),
                   full(n, P_WIDTH), full(n, G_WIDTH)],
        out_shape=[
            jax.ShapeDtypeStruct((n, 512), F32),
            jax.ShapeDtypeStruct((n, 512), F32),
            jax.ShapeDtypeStruct((n, 512), F32),
            jax.ShapeDtypeStruct((n, 512), BF16),
            jax.ShapeDtypeStruct((n, P_WIDTH), F32),
            jax.ShapeDtypeStruct((n, G_WIDTH), F32),
        ],
        compiler_params=_cparams(("arbitrary",)),
        name="mix_in_sample",
    )(x, prm["g_mix"], prm["w_in"], prm["head_ones"], prm["qg"], prm["kg"], cos, sin,
      prm["gn_g"], prm["gn_b"], prm["gw0"], prm["gb0"], prm["pool_bd"], prm["pool_scale"], hist)


def _subln(o, subg, lam_init):
    ms = jnp.mean(o * o, axis=-1, keepdims=True)
    return o * lax.rsqrt(ms + EPS) * subg * (1.0 - lam_init)


def _attn_prompt_kernel(qi_tab, ki_tab, q_ref, kt_ref, v_ref, lq1_ref, lk1_ref, lq2_ref, lk2_ref,
                        subg_ref, o_ref, qs_ref, m_ref, l_ref, acc_ref, *, tq, lam_init):
    step = pl.program_id(2)
    qi = qi_tab[step]
    ki = ki_tab[step]

    @pl.when(ki == 0)
    def _():
        q = q_ref[...]
        lane = lax.broadcasted_iota(jnp.int32, q.shape, 1)
        zero = jnp.zeros_like(q)
        qs_ref[0:tq, :] = jnp.where(lane < HEAD_DIM, q, zero)
        qs_ref[tq:2 * tq, :] = jnp.where(lane >= HEAD_DIM, q, zero)
        m_ref[...] = jnp.full(m_ref.shape, -jnp.inf, F32)
        l_ref[...] = jnp.zeros(l_ref.shape, F32)
        acc_ref[...] = jnp.zeros(acc_ref.shape, F32)

    def update(masked):
        s = jnp.dot(qs_ref[...], kt_ref[0].astype(BF16), preferred_element_type=F32)
        if masked:
            r = lax.broadcasted_iota(jnp.int32, s.shape, 0) % tq
            c = lax.broadcasted_iota(jnp.int32, s.shape, 1)
            s = jnp.where(c <= r, s, -jnp.inf)
        m_prev = m_ref[...]
        m_new = jnp.maximum(m_prev, jnp.max(s, axis=1, keepdims=True))
        alpha = jnp.exp(m_prev - m_new)
        p = jnp.exp(s - m_new[:, 0:1])
        l_ref[...] = alpha * l_ref[...] + jnp.sum(p, axis=1, keepdims=True)
        acc_ref[...] = alpha * acc_ref[...] + jnp.dot(
            p.astype(BF16), v_ref[...].astype(BF16), preferred_element_type=F32)
        m_ref[...] = m_new

    @pl.when(ki < qi)
    def _():
        update(False)

    @pl.when(ki == qi)
    def _():
        update(True)
        o = acc_ref[...] / l_ref[...]
        lam = _lam_value(lq1_ref[...], lk1_ref[...], lq2_ref[...], lk2_ref[...], lam_init)
        o = o[0:tq, :] - lam * o[tq:2 * tq, :]
        o_ref[...] = _subln(o, subg_ref[...], lam_init).astype(BF16)


def _attn_prompt(q, kt, v, prm, seq_len, layer):
    t = q.shape[0]
    nseq = t // seq_len
    tq = TQ
    nq = seq_len // tq
    pairs = [(a, b) for a in range(nq) for b in range(a + 1)]
    qi_tab = jnp.asarray([p[0] for p in pairs], jnp.int32)
    ki_tab = jnp.asarray([p[1] for p in pairs], jnp.int32)
    kern = functools.partial(_attn_prompt_kernel, tq=tq, lam_init=_lam_init(layer))
    vec = lambda w: pl.BlockSpec((1, w), lambda b, h, s, qt, kt_: (0, 0))
    grid_spec = pltpu.PrefetchScalarGridSpec(
        num_scalar_prefetch=2,
        grid=(nseq, A_HEADS, len(pairs)),
        in_specs=[
            pl.BlockSpec((tq, LANES), lambda b, h, s, qt, kt_: (b * nq + qt[s], h)),
            pl.BlockSpec((1, LANES, tq), lambda b, h, s, qt, kt_: (b, h, kt_[s])),
            pl.BlockSpec((tq, LANES), lambda b, h, s, qt, kt_: (b * nq + kt_[s], h)),
            vec(HEAD_DIM), vec(HEAD_DIM), vec(HEAD_DIM), vec(HEAD_DIM), vec(LANES),
        ],
        out_specs=pl.BlockSpec((tq, LANES), lambda b, h, s, qt, kt_: (b * nq + qt[s], h)),
        scratch_shapes=[
            pltpu.VMEM((2 * tq, LANES), BF16),
            pltpu.VMEM((2 * tq, LANES), F32),
            pltpu.VMEM((2 * tq, LANES), F32),
            pltpu.VMEM((2 * tq, LANES), F32),
        ],
    )
    return pl.pallas_call(
        kern,
        grid_spec=grid_spec,
        out_shape=jax.ShapeDtypeStruct((t, 512), BF16),
        compiler_params=_cparams(("arbitrary", "arbitrary", "arbitrary")),
        name="attn_prompt",
    )(qi_tab, ki_tab, q, kt, v, prm["lq1"], prm["lk1"], prm["lq2"], prm["lk2"], prm["subg"])


def _head_scores(qb, kt):
    prod = qb * kt
    return jnp.sum(prod.reshape(A_QK_HEADS, HEAD_DIM, LANES), axis=1)


def _attn_decode_kernel(pt_ref, qb_ref, knb_ref, vn_ref, lq1_ref, lk1_ref, lq2_ref, lk2_ref,
                        subg_ref, *rest, npg, lam_init):
    k_refs = rest[:npg]
    v_refs = rest[npg:2 * npg]
    o_ref, m_ref, l_ref, acc_ref = rest[2 * npg:]
    s_idx = pl.program_id(1)
    qb = qb_ref[0]

    @pl.when(s_idx == 0)
    def _():
        m_ref[...] = _head_scores(qb, knb_ref[0])
        l_ref[...] = jnp.ones(l_ref.shape, F32)
        acc_ref[...] = vn_ref[0]

    sc = jnp.concatenate([_head_scores(qb, k_refs[i][0, 0]) for i in range(npg)], axis=1)
    m_prev = m_ref[...]
    m_new = jnp.maximum(m_prev, jnp.max(sc, axis=1, keepdims=True))
    alpha = jnp.exp(m_prev - m_new)
    p = jnp.exp(sc - m_new[:, 0:1])
    l_ref[...] = alpha * l_ref[...] + jnp.sum(p, axis=1, keepdims=True)
    m_ref[...] = m_new
    head_of_row = lax.broadcasted_iota(jnp.int32, (A_QK_HEADS, A_HEADS * LANES), 0) // 2
    head_of_col = lax.broadcasted_iota(jnp.int32, (A_QK_HEADS, A_HEADS * LANES), 1) // LANES
    pv = jnp.zeros((A_QK_HEADS, LANES), F32)
    for i in range(npg):
        pi = p[:, i * LANES:(i + 1) * LANES]
        p_blk = jnp.where(head_of_row == head_of_col, jnp.concatenate([pi] * A_HEADS, axis=1), 0.0)
        v_heads = jnp.concatenate(
            [v_refs[i][0, 0, pl.ds(h, PAGE_SIZE, stride=A_HEADS), :] for h in range(A_HEADS)],
            axis=0)
        pv = pv + jnp.dot(p_blk.astype(BF16), v_heads.astype(BF16), preferred_element_type=F32)
    acc_ref[...] = alpha * acc_ref[...] + pv

    @pl.when(s_idx == pl.num_programs(1) - 1)
    def _():
        o = acc_ref[...] / l_ref[...]
        lam = _lam_value(lq1_ref[...], lk1_ref[...], lq2_ref[...], lk2_ref[...], lam_init)
        row = lax.broadcasted_iota(jnp.int32, o.shape, 0)
        oc = o * jnp.where(row % 2 == 0, 1.0, -lam)
        od = oc + pltpu.roll(oc, A_QK_HEADS - 1, 0)
        o_ref[0] = _subln(od, subg_ref[...], lam_init)


def _attn_decode(q_s, k_s, v_s, page_table, cache_kt, cache_vr, prm, layer):
    n = q_s.shape[0]
    n_pages = page_table.shape[1]
    npg = PAGES_PER_STEP
    steps = n_pages // npg
    qb = jnp.broadcast_to(q_s[:, :, None], (n, 512, LANES))
    knb = jnp.broadcast_to(k_s[:, :, None], (n, 512, LANES))
    vn8 = jnp.repeat(v_s.reshape(n, A_HEADS, LANES), 2, axis=1)
    kern = functools.partial(_attn_decode_kernel, npg=npg, lam_init=_lam_init(layer))
    vec = lambda w: pl.BlockSpec((1, w), lambda b, s, pt: (0, 0))
    seq3 = lambda r: pl.BlockSpec((1, r, LANES), lambda b, s, pt: (b, 0, 0))

    def page_spec(i):
        return pl.BlockSpec((1, 1, 512, LANES),
                            lambda b, s, pt, i=i: (layer, pt[b, s * npg + i], 0, 0))

    grid_spec = pltpu.PrefetchScalarGridSpec(
        num_scalar_prefetch=1,
        grid=(n, steps),
        in_specs=[seq3(512), seq3(512), seq3(A_QK_HEADS),
                  vec(HEAD_DIM), vec(HEAD_DIM), vec(HEAD_DIM), vec(HEAD_DIM), vec(LANES)]
                 + [page_spec(i) for i in range(npg)] + [page_spec(i) for i in range(npg)],
        out_specs=seq3(A_QK_HEADS),
        scratch_shapes=[pltpu.VMEM((A_QK_HEADS, LANES), F32)] * 3,
    )
    out = pl.pallas_call(
        kern,
        grid_spec=grid_spec,
        out_shape=jax.ShapeDtypeStruct((n, A_QK_HEADS, LANES), F32),
        compiler_params=_cparams(("arbitrary", "arbitrary")),
        name="attn_decode",
    )(page_table, qb, knb, vn8, prm["lq1"], prm["lk1"], prm["lq2"], prm["lk2"], prm["subg"],
      *([cache_kt] * npg), *([cache_vr] * npg))
    return out[:, 0::2, :].reshape(n, A_WIDTH).astype(BF16)


def _mix_out_kernel(a_ref, gp_ref, x_ref, woa_ref, wogp_ref, g_ref, *rest, routed):
    if routed:
        rw_ref, x1_ref, h_ref, route_ref = rest
    else:
        x1_ref, h_ref = rest
    y = jnp.dot(a_ref[...], woa_ref[...], preferred_element_type=F32)
    y = y + jnp.dot(gp_ref[...], wogp_ref[...], preferred_element_type=F32)
    x1 = x_ref[...] + y
    x1_ref[...] = x1
    ms = jnp.mean(x1 * x1, axis=-1, keepdims=True)
    h = x1 * lax.rsqrt(ms + EPS) * g_ref[...]
    h_ref[...] = h.astype(h_ref.dtype)
    if routed:
        logits = jnp.dot(h.astype(BF16), rw_ref[...], preferred_element_type=F32)
        lane = lax.broadcasted_iota(jnp.int32, logits.shape, 1)
        lg = jnp.where(lane < N_EXPERTS, logits, -jnp.inf)
        m1 = jnp.max(lg, axis=1, keepdims=True)
        i1 = jnp.min(jnp.where(lg == m1, lane, LANES), axis=1, keepdims=True)
        lg2 = jnp.where(lane == i1, -jnp.inf, lg)
        m2 = jnp.max(lg2, axis=1, keepdims=True)
        i2 = jnp.min(jnp.where(lg2 == m2, lane, LANES), axis=1, keepdims=True)
        e2 = jnp.exp(m2 - m1)
        den = 1.0 + e2
        route = jnp.where(lane == 0, i1.astype(F32),
                          jnp.where(lane == 1, i2.astype(F32),
                                    jnp.where(lane == 2, 1.0 / den,
                                              jnp.where(lane == 3, e2 / den, 0.0))))
        route_ref[...] = route


def _mix_out(a, gp, x, prm, tm, routed):
    t = x.shape[0]
    row = lambda w: pl.BlockSpec((tm, w), lambda i: (i, 0))
    in_specs = [row(512), row(512), row(D_MODEL), _const_spec((512, D_MODEL)),
                _const_spec((512, D_MODEL)), _const_spec((1, D_MODEL))]
    args = [a, gp, x, prm["wo_a"], prm["wo_gp"], prm["g_ffn"]]
    out_specs = [row(D_MODEL), row(D_MODEL)]
    out_shape = [jax.ShapeDtypeStruct((t, D_MODEL), F32),
                 jax.ShapeDtypeStruct((t, D_MODEL), F32 if routed else BF16)]
    if routed:
        in_specs.append(_const_spec((D_MODEL, LANES)))
        args.append(prm["router"])
        out_specs.append(row(LANES))
        out_shape.append(jax.ShapeDtypeStruct((t, LANES), F32))
    return pl.pallas_call(
        functools.partial(_mix_out_kernel, routed=routed),
        grid=(t // tm,),
        in_specs=in_specs, out_specs=out_specs, out_shape=out_shape,
        compiler_params=_cparams(("arbitrary",)),
        name="mix_out",
    )(*args)


def _silu(x):
    return x * (1.0 / (1.0 + jnp.exp(-x)))


def _ffn_kernel(h_ref, x_ref, wg_ref, wu_ref, wd_ref, o_ref, *, n_chunks):
    h = h_ref[...]
    f = wg_ref.shape[1]
    fc = f // n_chunks
    acc = x_ref[...]
    for c in range(n_chunks):
        gate = jnp.dot(h, wg_ref[:, c * fc:(c + 1) * fc], preferred_element_type=F32)
        up = jnp.dot(h, wu_ref[:, c * fc:(c + 1) * fc], preferred_element_type=F32)
        act = (_silu(gate) * up).astype(BF16)
        acc = acc + jnp.dot(act, wd_ref[c * fc:(c + 1) * fc, :], preferred_element_type=F32)
    o_ref[...] = acc


def _ffn(h, x, prm, tm):
    t = x.shape[0]
    f = prm["ffn_wg"].shape[1]
    row = lambda w: pl.BlockSpec((tm, w), lambda i: (i, 0))
    resident = lambda s: pl.BlockSpec(s, lambda i: (0, 0), pipeline_mode=pl.Buffered(1))
    return pl.pallas_call(
        functools.partial(_ffn_kernel, n_chunks=2),
        grid=(t // tm,),
        in_specs=[row(D_MODEL), row(D_MODEL), resident((D_MODEL, f)), resident((D_MODEL, f)),
                  resident((f, D_MODEL))],
        out_specs=row(D_MODEL),
        out_shape=jax.ShapeDtypeStruct((t, D_MODEL), F32),
        compiler_params=_cparams(("arbitrary",)),
        name="ffn_dense",
    )(h, x, prm["ffn_wg"], prm["ffn_wu"], prm["ffn_wd"])


def _gather_rows_kernel(idx_ref, src_ref, o_ref, sem, *, rows):
    base = pl.program_id(0) * rows

    def issue(r, carry):
        tok = idx_ref[base + r]
        pltpu.make_async_copy(src_ref.at[pl.ds(tok, 1)], o_ref.at[pl.ds(r, 1)], sem).start()
        return carry

    lax.fori_loop(0, rows, issue, 0)
    pltpu.make_async_copy(src_ref.at[pl.ds(0, rows)], o_ref, sem).wait()


def _gather_rows(idx, src, rows):
    n = idx.shape[0]
    grid_spec = pltpu.PrefetchScalarGridSpec(
        num_scalar_prefetch=1,
        grid=(n // rows,),
        in_specs=[pl.BlockSpec(memory_space=pl.ANY)],
        out_specs=pl.BlockSpec((rows, D_MODEL), lambda i, idx_: (i, 0)),
        scratch_shapes=[pltpu.SemaphoreType.DMA],
    )
    return pl.pallas_call(
        functools.partial(_gather_rows_kernel, rows=rows),
        grid_spec=grid_spec,
        out_shape=jax.ShapeDtypeStruct((n, D_MODEL), src.dtype),
        compiler_params=_cparams(("arbitrary",)),
        name="moe_gather",
    )(idx, src)


def _moe_ffn_kernel(te_ref, tv_ref, xs_ref, wg_ref, wu_ref, wd_ref, o_ref, xb_ref):
    i = pl.program_id(0)
    j = pl.program_id(1)
    valid = tv_ref[i] != 0

    @pl.when(j == 0)
    def _():
        xb_ref[...] = xs_ref[...].astype(BF16)
        o_ref[...] = jnp.zeros(o_ref.shape, F32)

    @pl.when(valid)
    def _():
        xb = xb_ref[...]
        gate = jnp.dot(xb, wg_ref[0], preferred_element_type=F32)
        up = jnp.dot(xb, wu_ref[0], preferred_element_type=F32)
        act = (_silu(gate) * up).astype(BF16)
        o_ref[...] += jnp.dot(act, wd_ref[0], preferred_element_type=F32)


def _moe_ffn(xs, tile_expert, tile_valid, prm, tm):
    n = xs.shape[0]
    f = prm["moe_wg"].shape[2]
    tf = TF_MOE
    grid_spec = pltpu.PrefetchScalarGridSpec(
        num_scalar_prefetch=2,
        grid=(n // tm, f // tf),
        in_specs=[
            pl.BlockSpec((tm, D_MODEL), lambda i, j, te, tv: (i, 0)),
            pl.BlockSpec((1, D_MODEL, tf), lambda i, j, te, tv: (te[i], 0, j)),
            pl.BlockSpec((1, D_MODEL, tf), lambda i, j, te, tv: (te[i], 0, j)),
            pl.BlockSpec((1, tf, D_MODEL), lambda i, j, te, tv: (te[i], j, 0)),
        ],
        out_specs=pl.BlockSpec((tm, D_MODEL), lambda i, j, te, tv: (i, 0)),
        scratch_shapes=[pltpu.VMEM((tm, D_MODEL), BF16)],
    )
    return pl.pallas_call(
        _moe_ffn_kernel,
        grid_spec=grid_spec,
        out_shape=jax.ShapeDtypeStruct((n, D_MODEL), F32),
        compiler_params=_cparams(("arbitrary", "arbitrary")),
        name="moe_ffn",
    )(tile_expert, tile_valid, xs, prm["moe_wg"], prm["moe_wu"], prm["moe_wd"])


def _moe_combine_kernel(d0_ref, d1_ref, ys_ref, x_ref, route_ref, o_ref, b0_ref, b1_ref, sem,
                        *, rows):
    base = pl.program_id(0) * rows

    def issue(r, carry):
        pltpu.make_async_copy(ys_ref.at[pl.ds(d0_ref[base + r], 1)], b0_ref.at[pl.ds(r, 1)],
                              sem.at[0]).start()
        pltpu.make_async_copy(ys_ref.at[pl.ds(d1_ref[base + r], 1)], b1_ref.at[pl.ds(r, 1)],
                              sem.at[1]).start()
        return carry

    lax.fori_loop(0, rows, issue, 0)
    pltpu.make_async_copy(ys_ref.at[pl.ds(0, rows)], b0_ref, sem.at[0]).wait()
    pltpu.make_async_copy(ys_ref.at[pl.ds(0, rows)], b1_ref, sem.at[1]).wait()
    route = route_ref[...]
    y = route[:, 2:3] * b0_ref[...] + route[:, 3:4] * b1_ref[...]
    o_ref[...] = x_ref[...] + y


def _moe_combine(dest0, dest1, ys, x, route, rows):
    t = x.shape[0]
    grid_spec = pltpu.PrefetchScalarGridSpec(
        num_scalar_prefetch=2,
        grid=(t // rows,),
        in_specs=[
            pl.BlockSpec(memory_space=pl.ANY),
            pl.BlockSpec((rows, D_MODEL), lambda i, a, b: (i, 0)),
            pl.BlockSpec((rows, LANES), lambda i, a, b: (i, 0)),
        ],
        out_specs=pl.BlockSpec((rows, D_MODEL), lambda i, a, b: (i, 0)),
        scratch_shapes=[pltpu.VMEM((rows, D_MODEL), F32), pltpu.VMEM((rows, D_MODEL), F32),
                        pltpu.SemaphoreType.DMA((2,))],
    )
    return pl.pallas_call(
        functools.partial(_moe_combine_kernel, rows=rows),
        grid_spec=grid_spec,
        out_shape=jax.ShapeDtypeStruct((t, D_MODEL), F32),
        compiler_params=_cparams(("arbitrary",)),
        name="moe_combine",
    )(dest0, dest1, ys, x, route)


def _moe(h, x, route, prm, tm, rows):
    t = h.shape[0]
    n_pairs = t * TOP_K
    n_tiles = n_pairs // tm + N_EXPERTS
    expert = route[:, 0:TOP_K].astype(jnp.int32).reshape(n_pairs)
    onehot = (expert[:, None] == jnp.arange(N_EXPERTS, dtype=jnp.int32)[None, :]).astype(jnp.int32)
    csum = jnp.cumsum(onehot, axis=0)
    rank = jnp.sum(csum * onehot, axis=1) - 1
    counts = csum[-1]
    padded = ((counts + tm - 1) // tm) * tm
    ends = jnp.cumsum(padded)
    starts = ends - padded
    dest = jnp.sum(onehot * starts[None, :], axis=1) + rank
    token = jnp.arange(n_pairs, dtype=jnp.int32) // TOP_K
    src = jnp.zeros((n_tiles * tm,), jnp.int32).at[dest].set(token)
    tile_start = jnp.arange(n_tiles, dtype=jnp.int32) * tm
    tile_expert = jnp.minimum(
        jnp.sum((tile_start[:, None] >= ends[None, :]).astype(jnp.int32), axis=1), N_EXPERTS - 1)
    tile_valid = (tile_start < ends[-1]).astype(jnp.int32)
    xs = _gather_rows(src, h, rows)
    ys = _moe_ffn(xs, tile_expert, tile_valid, prm, tm)
    dest2 = dest.reshape(t, TOP_K)
    return _moe_combine(dest2[:, 0], dest2[:, 1], ys, x, route, rows)


def _rope_tables(pos):
    half = HEAD_DIM // 2
    inv = ROPE_THETA ** (-jnp.arange(half, dtype=F32) / half)
    ang = pos.astype(F32)[:, None] * inv[None, :]
    cos = jnp.cos(ang)
    sin = jnp.sin(ang)
    cos_h = jnp.concatenate([cos, cos], axis=1)
    sin_h = jnp.concatenate([-sin, sin], axis=1)
    return jnp.tile(cos_h, (1, LANES // HEAD_DIM)), jnp.tile(sin_h, (1, LANES // HEAD_DIM))


def _layer_params(l, p):
    idx = jnp.arange(512)
    tril = jnp.tril(jnp.ones((CHUNK, CHUNK), F32))
    ws = p["gmlp_ws"][l] * tril[None]
    pool_bd = jnp.zeros((P_WIDTH, P_WIDTH), F32)
    for g in range(len(POOL_WINDOWS)):
        pool_bd = pool_bd.at[g * P_GC:(g + 1) * P_GC, g * P_GC:(g + 1) * P_GC].set(p["pool_w"][l, g])
    prm = {
        "g_mix": p["norm_mix_g"][l][None, :],
        "w_in": p["w_in"][l].astype(BF16),
        "head_ones": ((idx[:, None] // HEAD_DIM) == (idx[None, :] // HEAD_DIM)).astype(BF16),
        "qg": jnp.tile(p["q_norm_g"][l], A_QK_HEADS)[None, :],
        "kg": jnp.tile(p["k_norm_g"][l], A_QK_HEADS)[None, :],
        "gn_g": p["gmlp_norm_g"][l][None, :],
        "gn_b": p["gmlp_norm_b"][l][None, :],
        "gw_cat": jnp.concatenate([ws[g] for g in range(G_HEADS)], axis=1).astype(BF16),
        "gb_full": jnp.repeat(p["gmlp_bs"][l].T, HEAD_DIM, axis=1),
        "gw0": jnp.repeat(p["gmlp_ws"][l][:, 0, 0], HEAD_DIM)[None, :].astype(BF16).astype(F32),
        "gb0": jnp.repeat(p["gmlp_bs"][l][:, 0], HEAD_DIM)[None, :],
        "pool_bd": pool_bd.astype(BF16),
        "pool_scale": p["pool_scale"][l][None, :],
        "lq1": p["lam_q1"][l][None, :], "lk1": p["lam_k1"][l][None, :],
        "lq2": p["lam_q2"][l][None, :], "lk2": p["lam_k2"][l][None, :],
        "subg": p["subln_g"][l][None, :],
        "wo_a": p["w_out"][l][:A_WIDTH].astype(BF16),
        "wo_gp": p["w_out"][l][A_WIDTH:].astype(BF16),
        "g_ffn": p["norm_ffn_g"][l][None, :],
    }
    j = l // 2
    if l % 2 == 0:
        prm["ffn_wg"] = p["ffn_w_gate"][j].astype(BF16)
        prm["ffn_wu"] = p["ffn_w_up"][j].astype(BF16)
        prm["ffn_wd"] = p["ffn_w_down"][j].astype(BF16)
    else:
        prm["router"] = jnp.pad(p["router_w"][j], ((0, 0), (0, LANES - N_EXPERTS))).astype(BF16)
        prm["moe_wg"] = p["moe_w_gate"][j].astype(BF16)
        prm["moe_wu"] = p["moe_w_up"][j].astype(BF16)
        prm["moe_wd"] = p["moe_w_down"][j].astype(BF16)
    return prm


def kernel(x_prompt, x_sample, cache_k, cache_v, state_pool, page_table, norm_mix_g, w_in, q_norm_g, k_norm_g, lam_q1, lam_k1, lam_q2, lam_k2, subln_g, gmlp_norm_g, gmlp_norm_b, gmlp_ws, gmlp_bs, pool_w, pool_scale, w_out, norm_ffn_g, ffn_w_gate, ffn_w_up, ffn_w_down, router_w, moe_w_gate, moe_w_up, moe_w_down):
    params = dict(norm_mix_g=norm_mix_g, w_in=w_in, q_norm_g=q_norm_g, k_norm_g=k_norm_g,
                  lam_q1=lam_q1, lam_k1=lam_k1, lam_q2=lam_q2, lam_k2=lam_k2, subln_g=subln_g,
                  gmlp_norm_g=gmlp_norm_g, gmlp_norm_b=gmlp_norm_b, gmlp_ws=gmlp_ws,
                  gmlp_bs=gmlp_bs, pool_w=pool_w, pool_scale=pool_scale, w_out=w_out,
                  norm_ffn_g=norm_ffn_g, ffn_w_gate=ffn_w_gate, ffn_w_up=ffn_w_up,
                  ffn_w_down=ffn_w_down, router_w=router_w, moe_w_gate=moe_w_gate,
                  moe_w_up=moe_w_up, moe_w_down=moe_w_down)
    batch, seq_len, _ = x_prompt.shape
    n_dec, dec_seq, _ = x_sample.shape
    assert dec_seq == 1
    depth, n_pool = cache_k.shape[0], cache_k.shape[1]
    past_len = page_table.shape[1] * PAGE_SIZE
    t = batch * seq_len

    cos_p, sin_p = _rope_tables(jnp.arange(seq_len))
    cos_s, sin_s = _rope_tables(jnp.full((n_dec,), past_len, jnp.int32))
    cache_kt = cache_k.transpose(0, 1, 3, 4, 2).reshape(depth, n_pool, 512, PAGE_SIZE)
    cache_vr = cache_v.reshape(depth, n_pool, PAGE_SIZE * A_HEADS, LANES)

    xp = x_prompt.reshape(t, D_MODEL)
    xs = x_sample.reshape(n_dec, D_MODEL)
    kp_l, vp_l, pp_l, ks_l, vs_l, ps_l, gs_l = [], [], [], [], [], [], []
    for l in range(depth):
        prm = _layer_params(l, params)
        routed = l % 2 == 1

        q, kt, v, gp, px_last = _mix_in_prompt(xp, prm, cos_p, sin_p, seq_len)
        a = _attn_prompt(q, kt, v, prm, seq_len, l)
        if routed:
            xp1, hp, route_p = _mix_out(a, gp, xp, prm, TM_PROMPT, True)
        else:
            xp1, hp = _mix_out(a, gp, xp, prm, TM_PROMPT, False)
        kp_l.append(kt.reshape(batch, A_QK_HEADS, HEAD_DIM, seq_len).transpose(0, 3, 1, 2))
        vp_l.append(v.reshape(batch, seq_len, A_HEADS, 2 * HEAD_DIM))
        pp_l.append(px_last.reshape(batch, TM_PROMPT, P_WIDTH)[:, TM_PROMPT - POOL_HIST:, :])

        hist = state_pool[l].transpose(1, 0, 2)
        q_s, k_s, v_s, gp_s, px_s, gvn_s = _mix_in_sample(xs, prm, cos_s, sin_s, hist)
        a_s = _attn_decode(q_s, k_s, v_s, page_table, cache_kt, cache_vr, prm, l)
        if routed:
            xs1, hs, route_s = _mix_out(a_s, gp_s, xs, prm, n_dec, True)
        else:
            xs1, hs = _mix_out(a_s, gp_s, xs, prm, n_dec, False)
        ks_l.append(k_s.reshape(n_dec, 1, A_QK_HEADS, HEAD_DIM))
        vs_l.append(v_s.reshape(n_dec, 1, A_HEADS, 2 * HEAD_DIM))
        ps_l.append(jnp.concatenate([state_pool[l][:, 1:, :], px_s[:, None, :]], axis=1))
        gs_l.append(gvn_s.reshape(n_dec, 1, G_HEADS, HEAD_DIM))

        if routed:
            xp = _moe(hp, xp1, route_p, prm, TM_MOE, 256)
            xs = _moe(hs, xs1, route_s, prm, n_dec, n_dec)
        else:
            xp = _ffn(hp, xp1, prm, TM_PROMPT)
            xs = _ffn(hs, xs1, prm, n_dec)

    return (xp.reshape(batch, seq_len, D_MODEL), xs.reshape(n_dec, 1, D_MODEL),
            jnp.stack(kp_l), jnp.stack(vp_l), jnp.stack(pp_l), jnp.stack(ks_l), jnp.stack(vs_l),
            jnp.stack(ps_l), jnp.stack(gs_l))
```

```python
import functools
import math

import jax
import jax.numpy as jnp
from jax import lax
from jax.experimental import pallas as pl
from jax.experimental.pallas import tpu as pltpu

F32 = jnp.float32
BF16 = jnp.bfloat16

D_MODEL = 1024
HEAD_DIM = 64
A_HEADS = 4
A_QK_HEADS = 8
A_WIDTH = 512
G_WIDTH = 256
G_HEADS = 4
CHUNK = 128
P_WIDTH = 256
POOL_WINDOWS = (2, 4, 8, 16)
P_GC = 64
POOL_HIST = 15
D_IN = 2304
ROPE_THETA = 10000.0
N_EXPERTS = 8
TOP_K = 2
EPS = 1e-6
PAGE_SIZE = 128

LANES = 128
VMEM_LIMIT = 56 * 1024 * 1024

TM_PROMPT = 512
TQ = 1024
TM_MOE = 512
TF_MOE = 896
PAGES_PER_STEP = 8
DMA_UNROLL = 8


def _cparams(sem):
    return pltpu.CompilerParams(dimension_semantics=sem, vmem_limit_bytes=VMEM_LIMIT)


def _lam_init(layer):
    return 0.8 - 0.6 * math.exp(-0.3 * layer)


def _lam_value(lq1, lk1, lq2, lk2, lam_init):
    a = jnp.sum(lq1 * lk1, axis=-1, keepdims=True)
    b = jnp.sum(lq2 * lk2, axis=-1, keepdims=True)
    return jnp.exp(a) - jnp.exp(b) + lam_init


def _swap_halves(x):
    lane = lax.broadcasted_iota(jnp.int32, x.shape, 1)
    first_half = (lane % HEAD_DIM) < (HEAD_DIM // 2)
    return jnp.where(first_half, pltpu.roll(x, LANES - HEAD_DIM // 2, 1),
                     pltpu.roll(x, HEAD_DIM // 2, 1))


def _head_norm_rope(t, head_ones, gain, cos, sin):
    sq = t * t
    ss = jnp.dot(sq.astype(BF16), head_ones, preferred_element_type=F32)
    ss = ss + jnp.dot((sq - sq.astype(BF16).astype(F32)).astype(BF16), head_ones,
                      preferred_element_type=F32)
    tn = t * lax.rsqrt(ss * (1.0 / HEAD_DIM) + EPS) * gain
    cols = []
    for c in range(t.shape[1] // LANES):
        tc = tn[:, c * LANES:(c + 1) * LANES]
        cols.append(tc * cos + _swap_halves(tc) * sin)
    return jnp.concatenate(cols, axis=1)


def _project(x, g_mix, w_in, head_ones, qg, kg, cos, sin, gn_g, gn_b):
    ms = jnp.mean(x * x, axis=-1, keepdims=True)
    h = x * lax.rsqrt(ms + EPS) * g_mix
    z = jnp.dot(h.astype(BF16), w_in, preferred_element_type=F32)
    q = _head_norm_rope(z[:, 0:512], head_ones, qg, cos, sin)
    k = _head_norm_rope(z[:, 512:1024], head_ones, kg, cos, sin)
    v = z[:, 1024:1536]
    gu = z[:, 1536:1792]
    gv = z[:, 1792:2048]
    px = z[:, 2048:2304]
    mu = jnp.mean(gv, axis=-1, keepdims=True)
    gc = gv - mu
    var = jnp.mean(gc * gc, axis=-1, keepdims=True)
    gvn = gc * lax.rsqrt(var + EPS) * gn_g + gn_b
    return q, k, v, gu, gvn, px


def _pool_window_select(s2, s4, s8, s16):
    lane = lax.broadcasted_iota(jnp.int32, s2.shape, 1)
    return jnp.where(lane < P_GC, s2,
                     jnp.where(lane < 2 * P_GC, s4, jnp.where(lane < 3 * P_GC, s8, s16)))


def _pool_window_sizes(shape):
    lane = lax.broadcasted_iota(jnp.int32, shape, 1)
    return jnp.where(lane < P_GC, 2.0,
                     jnp.where(lane < 2 * P_GC, 4.0,
                               jnp.where(lane < 3 * P_GC, 8.0, 16.0))).astype(F32)


def _head_norm_rope_t(t, gain, cos, sin):
    n = t.shape[1]
    half = HEAD_DIM // 2
    th = t.reshape(A_QK_HEADS, HEAD_DIM, n)
    ms = jnp.mean(th * th, axis=1, keepdims=True)
    tn = (th * lax.rsqrt(ms + EPS)).reshape(A_QK_HEADS * HEAD_DIM, n) * gain
    tn = tn.reshape(A_QK_HEADS, HEAD_DIM, n)
    x1 = tn[:, 0:half, :]
    x2 = tn[:, half:HEAD_DIM, :]
    out = jnp.concatenate([x1 * cos[None] - x2 * sin[None], x2 * cos[None] + x1 * sin[None]],
                          axis=1)
    return out.reshape(A_QK_HEADS * HEAD_DIM, n)


def _mix_in_prompt_kernel(x_ref, gmix_ref, wqkvt_ref, wrest_ref, qg_ref, kg_ref, cos_ref, sin_ref,
                          gng_ref, gnb_ref, gw_ref, gb_ref, pw_ref, ps_ref,
                          qt_ref, kt_ref, vt_ref, v_ref, gp_ref, pxl_ref, ext_ref,
                          *, tm, tiles_per_seq):
    i = pl.program_id(0)
    x = x_ref[...]
    ms = jnp.mean(x * x, axis=-1, keepdims=True)
    h = (x * lax.rsqrt(ms + EPS) * gmix_ref[...]).astype(BF16)
    zt = lax.dot_general(wqkvt_ref[...], h, (((1,), (1,)), ((), ())), preferred_element_type=F32)
    z = jnp.dot(h, wrest_ref[...], preferred_element_type=F32)
    cos = cos_ref[...]
    sin = sin_ref[...]
    qt = _head_norm_rope_t(zt[0:512, :], qg_ref[...], cos, sin)
    qt_ref[0] = (qt * (HEAD_DIM ** -0.5)).astype(BF16)
    kt_ref[0] = _head_norm_rope_t(zt[512:1024, :], kg_ref[...], cos, sin)
    vt_ref[0] = zt[1024:1536, :].astype(BF16)
    v_ref[...] = z[:, 0:512]
    gu = z[:, 512:768]
    gv = z[:, 768:1024]
    px = z[:, 1024:1280]
    mu = jnp.mean(gv, axis=-1, keepdims=True)
    gc = gv - mu
    var = jnp.mean(gc * gc, axis=-1, keepdims=True)
    gvn = gc * lax.rsqrt(var + EPS) * gng_ref[...] + gnb_ref[...]
    pxl_ref[...] = px

    lane = lax.broadcasted_iota(jnp.int32, (CHUNK, G_WIDTH), 1)
    for c in range(tm // CHUNK):
        vn_c = gvn[c * CHUNK:(c + 1) * CHUNK, :]
        rhs = jnp.concatenate(
            [jnp.where((lane // HEAD_DIM) == g, vn_c, 0.0).astype(BF16) for g in range(G_HEADS)],
            axis=0)
        mixed = jnp.dot(gw_ref[...], rhs, preferred_element_type=F32) + gb_ref[...]
        gp_ref[c * CHUNK:(c + 1) * CHUNK, 0:G_WIDTH] = (
            gu[c * CHUNK:(c + 1) * CHUNK, :] * mixed).astype(BF16)

    seq_tile = i % tiles_per_seq

    @pl.when(seq_tile == 0)
    def _():
        ext_ref[0:16, :] = jnp.zeros((16, P_WIDTH), F32)

    @pl.when(seq_tile != 0)
    def _():
        ext_ref[0:16, :] = ext_ref[tm:tm + 16, :]

    ext_ref[16:16 + tm, :] = px

    def back(kk):
        return ext_ref[16 - kk:16 - kk + tm, :]

    s2 = px + back(1)
    s4 = s2 + back(2) + back(3)
    s8 = s4 + back(4) + back(5) + back(6) + back(7)
    s16 = s8
    for kk in range(8, 16):
        s16 = s16 + back(kk)
    tot = _pool_window_select(s2, s4, s8, s16)
    pos = (seq_tile * tm + lax.broadcasted_iota(jnp.int32, (tm, P_WIDTH), 0)).astype(F32)
    cnt = jnp.minimum(_pool_window_sizes((tm, P_WIDTH)), pos + 1.0)
    d = tot / cnt - px
    p = jnp.dot(d.astype(BF16), pw_ref[...], preferred_element_type=F32) * ps_ref[...]
    gp_ref[:, G_WIDTH:G_WIDTH + P_WIDTH] = p.astype(BF16)


def _mix_in_sample_kernel(x_ref, gmix_ref, win_ref, ones_ref, qg_ref, kg_ref, cos_ref, sin_ref,
                          gng_ref, gnb_ref, gw0_ref, gb0_ref, pw_ref, ps_ref, hist_ref,
                          q_ref, k_ref, v_ref, gp_ref, px_ref, gvn_ref):
    q, k, v, gu, gvn, px = _project(
        x_ref[...], gmix_ref[...], win_ref[...], ones_ref[...], qg_ref[...], kg_ref[...],
        cos_ref[...], sin_ref[...], gng_ref[...], gnb_ref[...])
    q_ref[...] = q * (HEAD_DIM ** -0.5)
    k_ref[...] = k
    v_ref[...] = v
    px_ref[...] = px
    gvn_ref[...] = gvn
    mixed = gw0_ref[...] * gvn + gb0_ref[...]
    gp_ref[:, 0:G_WIDTH] = (gu * mixed).astype(BF16)

    def tail(n):
        acc = px
        for r in range(POOL_HIST - n, POOL_HIST):
            acc = acc + hist_ref[r]
        return acc

    tot = _pool_window_select(tail(1), tail(3), tail(7), tail(15))
    d = tot / _pool_window_sizes(px.shape) - px
    p = jnp.dot(d.astype(BF16), pw_ref[...], preferred_element_type=F32) * ps_ref[...]
    gp_ref[:, G_WIDTH:G_WIDTH + P_WIDTH] = p.astype(BF16)


def _const_spec(shape):
    return pl.BlockSpec(shape, lambda *_: (0,) * len(shape))


def _mix_in_prompt(x, prm, cos, sin, seq_len):
    t = x.shape[0]
    tm = TM_PROMPT
    tps = seq_len // tm
    nseq = t // seq_len
    kern = functools.partial(_mix_in_prompt_kernel, tm=tm, tiles_per_seq=tps)
    row = lambda w: pl.BlockSpec((tm, w), lambda i: (i, 0))
    seq_t = pl.BlockSpec((1, 512, tm), lambda i: (i // tps, 0, i % tps))
    return pl.pallas_call(
        kern,
        grid=(t // tm,),
        in_specs=[
            row(D_MODEL), _const_spec((1, D_MODEL)), _const_spec((3 * 512, D_MODEL)),
            _const_spec((D_MODEL, D_IN - 1024)), _const_spec((512, 1)), _const_spec((512, 1)),
            pl.BlockSpec((HEAD_DIM // 2, tm), lambda i: (0, i % tps)),
            pl.BlockSpec((HEAD_DIM // 2, tm), lambda i: (0, i % tps)),
            _const_spec((1, G_WIDTH)), _const_spec((1, G_WIDTH)),
            _const_spec((CHUNK, G_HEADS * CHUNK)), _const_spec((CHUNK, G_WIDTH)),
            _const_spec((P_WIDTH, P_WIDTH)), _const_spec((1, P_WIDTH)),
        ],
        out_specs=[
            seq_t, seq_t, seq_t, row(512), row(512),
            pl.BlockSpec((tm, P_WIDTH), lambda i: (i // tps, 0)),
        ],
        out_shape=[
            jax.ShapeDtypeStruct((nseq, 512, seq_len), BF16),
            jax.ShapeDtypeStruct((nseq, 512, seq_len), F32),
            jax.ShapeDtypeStruct((nseq, 512, seq_len), BF16),
            jax.ShapeDtypeStruct((t, 512), F32),
            jax.ShapeDtypeStruct((t, 512), BF16),
            jax.ShapeDtypeStruct((nseq * tm, P_WIDTH), F32),
        ],
        scratch_shapes=[pltpu.VMEM((tm + 16, P_WIDTH), F32)],
        compiler_params=_cparams(("arbitrary",)),
        name="mix_in_prompt",
    )(x, prm["g_mix"], prm["w_qkv_t"], prm["w_rest"], prm["qg_col"], prm["kg_col"], cos, sin,
      prm["gn_g"], prm["gn_b"], prm["gw_cat"], prm["gb_full"], prm["pool_bd"], prm["pool_scale"])


def _mix_in_sample(x, prm, cos, sin, hist):
    n = x.shape[0]
    full = lambda *s: _const_spec(s)
    return pl.pallas_call(
        _mix_in_sample_kernel,
        grid=(1,),
        in_specs=[
            full(n, D_MODEL), full(1, D_MODEL), full(D_MODEL, D_IN), full(512, 512),
            full(1, 512), full(1, 512), full(n, LANES), full(n, LANES),
            full(1, G_WIDTH), full(1, G_WIDTH), full(1, G_WIDTH), full(1, G_WIDTH),
            full(P_WIDTH, P_WIDTH), full(1, P_WIDTH), full(POOL_HIST, n, P_WIDTH),
        ],
        out_specs=[full(n, 512), full(n, 512), full(n, 512), full(n, 512),
                   full(n, P_WIDTH), full(n, G_WIDTH)],
        out_shape=[
            jax.ShapeDtypeStruct((n, 512), F32),
            jax.ShapeDtypeStruct((n, 512), F32),
            jax.ShapeDtypeStruct((n, 512), F32),
            jax.ShapeDtypeStruct((n, 512), BF16),
            jax.ShapeDtypeStruct((n, P_WIDTH), F32),
            jax.ShapeDtypeStruct((n, G_WIDTH), F32),
        ],
        compiler_params=_cparams(("arbitrary",)),
        name="mix_in_sample",
    )(x, prm["g_mix"], prm["w_in"], prm["head_ones"], prm["qg"], prm["kg"], cos, sin,
      prm["gn_g"], prm["gn_b"], prm["gw0"], prm["gb0"], prm["pool_bd"], prm["pool_scale"], hist)


def _subln(o, subg, lam_init):
    ms = jnp.mean(o * o, axis=-1, keepdims=True)
    return o * lax.rsqrt(ms + EPS) * subg * (1.0 - lam_init)


def _attn_prompt_kernel(qi_tab, ki_tab, qt_ref, kt_ref, vt_ref, lq1_ref, lk1_ref, lq2_ref,
                        lk2_ref, subg_ref, o_ref, qs_ref, m_ref, l_ref, acc_ref, *, tq, lam_init):
    step = pl.program_id(2)
    qi = qi_tab[step]
    ki = ki_tab[step]

    @pl.when(ki == 0)
    def _():
        q = qt_ref[0]
        row = lax.broadcasted_iota(jnp.int32, q.shape, 0)
        zero = jnp.zeros_like(q)
        qs_ref[:, 0:tq] = jnp.where(row < HEAD_DIM, q, zero)
        qs_ref[:, tq:2 * tq] = jnp.where(row >= HEAD_DIM, q, zero)
        m_ref[...] = jnp.full(m_ref.shape, -jnp.inf, F32)
        l_ref[...] = jnp.zeros(l_ref.shape, F32)
        acc_ref[...] = jnp.zeros(acc_ref.shape, F32)

    def update(masked):
        s = lax.dot_general(kt_ref[0].astype(BF16), qs_ref[...], (((0,), (0,)), ((), ())),
                            preferred_element_type=F32)
        if masked:
            r = lax.broadcasted_iota(jnp.int32, s.shape, 0)
            c = lax.broadcasted_iota(jnp.int32, s.shape, 1) % tq
            s = jnp.where(r <= c, s, -jnp.inf)
        m_prev = m_ref[...]
        m_new = jnp.maximum(m_prev, jnp.max(s, axis=0, keepdims=True))
        alpha = jnp.exp(m_prev - m_new)
        p = jnp.exp(s - m_new[0:1, :])
        l_ref[...] = alpha * l_ref[...] + jnp.sum(p, axis=0, keepdims=True)
        acc_ref[...] = alpha[0:1, :] * acc_ref[...] + jnp.dot(
            vt_ref[0], p.astype(BF16), preferred_element_type=F32)
        m_ref[...] = m_new

    @pl.when(ki < qi)
    def _():
        update(False)

    @pl.when(ki == qi)
    def _():
        update(True)
        o = acc_ref[...] / l_ref[0:1, :]
        lam = _lam_value(lq1_ref[...], lk1_ref[...], lq2_ref[...], lk2_ref[...], lam_init)
        od = o[:, 0:tq] - lam * o[:, tq:2 * tq]
        ms = jnp.mean(od * od, axis=0, keepdims=True)
        o_ref[0] = (od * lax.rsqrt(ms + EPS) * subg_ref[...] * (1.0 - lam_init)).astype(BF16)


def _attn_prompt(qt, kt, vt, prm, layer):
    nseq, _, seq_len = qt.shape
    tq = TQ
    nq = seq_len // tq
    pairs = [(a, b) for a in range(nq) for b in range(a + 1)]
    qi_tab = jnp.asarray([p[0] for p in pairs], jnp.int32)
    ki_tab = jnp.asarray([p[1] for p in pairs], jnp.int32)
    kern = functools.partial(_attn_prompt_kernel, tq=tq, lam_init=_lam_init(layer))
    vec = lambda w: pl.BlockSpec((1, w), lambda b, h, s, qt_, kt_: (0, 0))
    q_tile = pl.BlockSpec((1, LANES, tq), lambda b, h, s, qt_, kt_: (b, h, qt_[s]))
    k_tile = pl.BlockSpec((1, LANES, tq), lambda b, h, s, qt_, kt_: (b, h, kt_[s]))
    grid_spec = pltpu.PrefetchScalarGridSpec(
        num_scalar_prefetch=2,
        grid=(nseq, A_HEADS, len(pairs)),
        in_specs=[
            q_tile, k_tile, k_tile,
            vec(HEAD_DIM), vec(HEAD_DIM), vec(HEAD_DIM), vec(HEAD_DIM),
            pl.BlockSpec((LANES, 1), lambda b, h, s, qt_, kt_: (0, 0)),
        ],
        out_specs=q_tile,
        scratch_shapes=[
            pltpu.VMEM((LANES, 2 * tq), BF16),
            pltpu.VMEM((8, 2 * tq), F32),
            pltpu.VMEM((8, 2 * tq), F32),
            pltpu.VMEM((LANES, 2 * tq), F32),
        ],
    )
    return pl.pallas_call(
        kern,
        grid_spec=grid_spec,
        out_shape=jax.ShapeDtypeStruct((nseq, 512, seq_len), BF16),
        compiler_params=_cparams(("arbitrary", "arbitrary", "arbitrary")),
        name="attn_prompt",
    )(qi_tab, ki_tab, qt, kt, vt, prm["lq1"], prm["lk1"], prm["lq2"], prm["lk2"],
      prm["subg_col"])


def _head_scores(qb, kt):
    prod = qb * kt
    return jnp.sum(prod.reshape(A_QK_HEADS, HEAD_DIM, LANES), axis=1)


def _attn_decode_kernel(pt_ref, qb_ref, knb_ref, vn_ref, lq1_ref, lk1_ref, lq2_ref, lk2_ref,
                        subg_ref, ck_ref, cv_ref, o_ref, buf_ref, sc_ref, sem,
                        *, npg, layer, lam_init):
    b = pl.program_id(0)
    n_seq = pl.num_programs(0)
    n_groups = pt_ref.shape[1] // npg

    def bf16_round(x):
        return x.astype(BF16).astype(F32)

    qb = bf16_round(qb_ref[0])

    def start_group(cache_ref, seq, grp, slot):
        for i in range(npg):
            page = pt_ref[seq, grp * npg + i]
            pltpu.make_async_copy(cache_ref.at[layer, page], buf_ref.at[slot, i],
                                  sem.at[slot]).start()

    def wait_group(slot):
        pltpu.make_async_copy(ck_ref.at[layer, pl.ds(0, npg)], buf_ref.at[slot],
                              sem.at[slot]).wait()

    @pl.when(b == 0)
    def _():
        start_group(ck_ref, 0, 0, 0)

    def key_pass(grp, m_run):
        slot = grp % 2
        last = grp == n_groups - 1

        @pl.when(jnp.logical_not(last))
        def _():
            start_group(ck_ref, b, grp + 1, 1 - slot)

        @pl.when(last)
        def _():
            start_group(cv_ref, b, 0, 1 - slot)

        wait_group(slot)
        sc = jnp.concatenate(
            [_head_scores(qb, bf16_round(buf_ref[slot, i])) for i in range(npg)], axis=1)
        sc_ref[grp] = sc
        return jnp.maximum(m_run, jnp.max(sc, axis=1, keepdims=True))

    s_new = _head_scores(qb_ref[0], knb_ref[0])
    m_fin = lax.fori_loop(0, n_groups, key_pass, s_new)

    def exp_pass(grp, l_run):
        p = jnp.exp(sc_ref[grp] - m_fin[:, 0:1])
        sc_ref[grp] = p
        return l_run + jnp.sum(p, axis=1, keepdims=True)

    p_new = jnp.exp(s_new - m_fin)
    l_fin = lax.fori_loop(0, n_groups, exp_pass, jnp.zeros((A_QK_HEADS, LANES), F32)) + p_new

    lam = _lam_value(lq1_ref[...], lk1_ref[...], lq2_ref[...], lk2_ref[...], lam_init)
    row8 = lax.broadcasted_iota(jnp.int32, (A_QK_HEADS, LANES), 0)
    coef = jnp.where(row8 % 2 == 0, 1.0, -lam)

    def diff_weights(p):
        n = p.shape[1] // LANES
        wc = p / jnp.tile(l_fin, (1, n)) * jnp.tile(coef, (1, n))
        wd = wc + pltpu.roll(wc, A_QK_HEADS - 1, 0)
        even = lax.broadcasted_iota(jnp.int32, wd.shape, 0) % 2 == 0
        return jnp.where(even, wd, 0.0)

    head_of_row = lax.broadcasted_iota(jnp.int32, (A_QK_HEADS, A_HEADS * LANES), 0) // 2
    head_of_col = lax.broadcasted_iota(jnp.int32, (A_QK_HEADS, A_HEADS * LANES), 1) // LANES
    slot_base = n_groups % 2

    def value_pass(grp, acc):
        slot = (slot_base + grp) % 2
        last = grp == n_groups - 1

        @pl.when(jnp.logical_not(last))
        def _():
            start_group(cv_ref, b, grp + 1, 1 - slot)

        @pl.when(last & (b + 1 < n_seq))
        def _():
            start_group(ck_ref, jnp.minimum(b + 1, n_seq - 1), 0, 1 - slot)

        wait_group(slot)
        w = diff_weights(sc_ref[grp])
        p_parts, v_parts = [], []
        for i in range(npg):
            wi = w[:, i * LANES:(i + 1) * LANES]
            p_parts.append(jnp.where(head_of_row == head_of_col,
                                     jnp.concatenate([wi] * A_HEADS, axis=1), 0.0).astype(BF16))
            for h in range(A_HEADS):
                v_parts.append(
                    buf_ref[slot, i, pl.ds(h, PAGE_SIZE, stride=A_HEADS), :].astype(BF16))
        return acc + jnp.dot(jnp.concatenate(p_parts, axis=1), jnp.concatenate(v_parts, axis=0),
                             preferred_element_type=F32)

    acc = lax.fori_loop(0, n_groups, value_pass, jnp.zeros((A_QK_HEADS, LANES), F32))
    acc = acc + diff_weights(p_new) * vn_ref[0]
    o_ref[0] = _subln(acc, subg_ref[...], lam_init)


def _attn_decode(q_s, k_s, v_s, page_table, cache_kt, cache_vr, prm, layer):
    n = q_s.shape[0]
    npg = PAGES_PER_STEP
    assert page_table.shape[1] % npg == 0
    n_groups = page_table.shape[1] // npg
    qb = jnp.broadcast_to(q_s[:, :, None], (n, 512, LANES))
    knb = jnp.broadcast_to(k_s[:, :, None], (n, 512, LANES))
    vn8 = jnp.repeat(v_s.reshape(n, A_HEADS, LANES), 2, axis=1)
    kern = functools.partial(_attn_decode_kernel, npg=npg, layer=layer,
                             lam_init=_lam_init(layer))
    vec = lambda w: pl.BlockSpec((1, w), lambda b, pt: (0, 0))
    seq3 = lambda r: pl.BlockSpec((1, r, LANES), lambda b, pt: (b, 0, 0))
    grid_spec = pltpu.PrefetchScalarGridSpec(
        num_scalar_prefetch=1,
        grid=(n,),
        in_specs=[seq3(512), seq3(512), seq3(A_QK_HEADS),
                  vec(HEAD_DIM), vec(HEAD_DIM), vec(HEAD_DIM), vec(HEAD_DIM), vec(LANES),
                  pl.BlockSpec(memory_space=pl.ANY), pl.BlockSpec(memory_space=pl.ANY)],
        out_specs=seq3(A_QK_HEADS),
        scratch_shapes=[pltpu.VMEM((2, npg, PAGE_SIZE * A_HEADS, LANES), F32),
                        pltpu.VMEM((n_groups, A_QK_HEADS, npg * PAGE_SIZE), F32),
                        pltpu.SemaphoreType.DMA((2,))],
    )
    out = pl.pallas_call(
        kern,
        grid_spec=grid_spec,
        out_shape=jax.ShapeDtypeStruct((n, A_QK_HEADS, LANES), F32),
        compiler_params=_cparams(("arbitrary",)),
        name="attn_decode",
    )(page_table, qb, knb, vn8, prm["lq1"], prm["lk1"], prm["lq2"], prm["lk2"], prm["subg"],
      cache_kt, cache_vr)
    return out[:, 0::2, :].reshape(n, A_WIDTH).astype(BF16)


def _mix_out_kernel(a_ref, gp_ref, x_ref, woa_ref, wogp_ref, g_ref, *rest, routed, a_transposed):
    if routed:
        rw_ref, x1_ref, h_ref, route_ref = rest
    else:
        x1_ref, h_ref = rest
    if a_transposed:
        y = lax.dot_general(a_ref[0], woa_ref[...], (((0,), (0,)), ((), ())),
                            preferred_element_type=F32)
    else:
        y = jnp.dot(a_ref[...], woa_ref[...], preferred_element_type=F32)
    y = y + jnp.dot(gp_ref[...], wogp_ref[...], preferred_element_type=F32)
    x1 = x_ref[...] + y
    x1_ref[...] = x1
    ms = jnp.mean(x1 * x1, axis=-1, keepdims=True)
    h = x1 * lax.rsqrt(ms + EPS) * g_ref[...]
    h_ref[...] = h.astype(h_ref.dtype)
    if routed:
        logits = jnp.dot(h.astype(BF16), rw_ref[...], preferred_element_type=F32)
        lane = lax.broadcasted_iota(jnp.int32, logits.shape, 1)
        lg = jnp.where(lane < N_EXPERTS, logits, -jnp.inf)
        m1 = jnp.max(lg, axis=1, keepdims=True)
        i1 = jnp.min(jnp.where(lg == m1, lane, LANES), axis=1, keepdims=True)
        lg2 = jnp.where(lane == i1, -jnp.inf, lg)
        m2 = jnp.max(lg2, axis=1, keepdims=True)
        i2 = jnp.min(jnp.where(lg2 == m2, lane, LANES), axis=1, keepdims=True)
        e2 = jnp.exp(m2 - m1)
        den = 1.0 + e2
        route = jnp.where(lane == 0, i1.astype(F32),
                          jnp.where(lane == 1, i2.astype(F32),
                                    jnp.where(lane == 2, 1.0 / den,
                                              jnp.where(lane == 3, e2 / den, 0.0))))
        route_ref[...] = route


def _mix_out(a, gp, x, prm, tm, routed):
    t = x.shape[0]
    row = lambda w: pl.BlockSpec((tm, w), lambda i: (i, 0))
    a_transposed = a.ndim == 3
    if a_transposed:
        tps = a.shape[2] // tm
        a_spec = pl.BlockSpec((1, 512, tm), lambda i: (i // tps, 0, i % tps))
    else:
        a_spec = row(512)
    in_specs = [a_spec, row(512), row(D_MODEL), _const_spec((512, D_MODEL)),
                _const_spec((512, D_MODEL)), _const_spec((1, D_MODEL))]
    args = [a, gp, x, prm["wo_a"], prm["wo_gp"], prm["g_ffn"]]
    out_specs = [row(D_MODEL), row(D_MODEL)]
    out_shape = [jax.ShapeDtypeStruct((t, D_MODEL), F32),
                 jax.ShapeDtypeStruct((t, D_MODEL), F32 if routed else BF16)]
    if routed:
        in_specs.append(_const_spec((D_MODEL, LANES)))
        args.append(prm["router"])
        out_specs.append(row(LANES))
        out_shape.append(jax.ShapeDtypeStruct((t, LANES), F32))
    return pl.pallas_call(
        functools.partial(_mix_out_kernel, routed=routed, a_transposed=a_transposed),
        grid=(t // tm,),
        in_specs=in_specs, out_specs=out_specs, out_shape=out_shape,
        compiler_params=_cparams(("arbitrary",)),
        name="mix_out",
    )(*args)


def _silu(x):
    return x * (1.0 / (1.0 + jnp.exp(-x)))


def _ffn_kernel(h_ref, x_ref, wg_ref, wu_ref, wd_ref, o_ref, *, n_chunks):
    h = h_ref[...]
    f = wg_ref.shape[1]
    fc = f // n_chunks
    acc = x_ref[...]
    for c in range(n_chunks):
        gate = jnp.dot(h, wg_ref[:, c * fc:(c + 1) * fc], preferred_element_type=F32)
        up = jnp.dot(h, wu_ref[:, c * fc:(c + 1) * fc], preferred_element_type=F32)
        act = (_silu(gate) * up).astype(BF16)
        acc = acc + jnp.dot(act, wd_ref[c * fc:(c + 1) * fc, :], preferred_element_type=F32)
    o_ref[...] = acc


def _ffn(h, x, prm, tm):
    t = x.shape[0]
    f = prm["ffn_wg"].shape[1]
    row = lambda w: pl.BlockSpec((tm, w), lambda i: (i, 0))
    resident = lambda s: pl.BlockSpec(s, lambda i: (0, 0), pipeline_mode=pl.Buffered(1))
    return pl.pallas_call(
        functools.partial(_ffn_kernel, n_chunks=2),
        grid=(t // tm,),
        in_specs=[row(D_MODEL), row(D_MODEL), resident((D_MODEL, f)), resident((D_MODEL, f)),
                  resident((f, D_MODEL))],
        out_specs=row(D_MODEL),
        out_shape=jax.ShapeDtypeStruct((t, D_MODEL), F32),
        compiler_params=_cparams(("arbitrary",)),
        name="ffn_dense",
    )(h, x, prm["ffn_wg"], prm["ffn_wu"], prm["ffn_wd"])


def _moe_ffn_kernel(te_ref, tv_ref, src_ref, h_ref, wg_ref, wu_ref, wd_ref, o_ref,
                    xg_ref, xb_ref, sem, *, tm):
    i = pl.program_id(0)
    j = pl.program_id(1)
    n_tiles = pl.num_programs(0)
    valid = tv_ref[i] != 0

    def start_gather(tile):
        slot = tile % 2
        base = tile * tm

        def issue(g, carry):
            for u in range(DMA_UNROLL):
                r = g * DMA_UNROLL + u
                pltpu.make_async_copy(h_ref.at[pl.ds(src_ref[base + r], 1)],
                                      xg_ref.at[slot, pl.ds(r, 1)],
                                      sem.at[slot]).start(priority=u % 2)
            return carry

        lax.fori_loop(0, tm // DMA_UNROLL, issue, 0)

    @pl.when((i == 0) & (j == 0) & valid)
    def _():
        start_gather(i)

    @pl.when((j == 0) & valid)
    def _():
        slot = i % 2
        pltpu.make_async_copy(h_ref.at[pl.ds(0, tm)], xg_ref.at[slot], sem.at[slot]).wait()
        xb_ref[...] = xg_ref[slot].astype(BF16)

    @pl.when(j == 0)
    def _():
        o_ref[...] = jnp.zeros(o_ref.shape, F32)

    @pl.when((j == 1) & (i + 1 < n_tiles))
    def _():
        @pl.when(tv_ref[jnp.minimum(i + 1, n_tiles - 1)] != 0)
        def _():
            start_gather(i + 1)

    @pl.when(valid)
    def _():
        xb = xb_ref[...]
        gate = jnp.dot(xb, wg_ref[0], preferred_element_type=F32)
        up = jnp.dot(xb, wu_ref[0], preferred_element_type=F32)
        act = (_silu(gate) * up).astype(BF16)
        o_ref[...] += jnp.dot(act, wd_ref[0], preferred_element_type=F32)


def _moe_ffn(h, src, tile_expert, tile_valid, prm, tm):
    n = src.shape[0]
    f = prm["moe_wg"].shape[2]
    tf = TF_MOE
    assert f // tf >= 2
    grid_spec = pltpu.PrefetchScalarGridSpec(
        num_scalar_prefetch=3,
        grid=(n // tm, f // tf),
        in_specs=[
            pl.BlockSpec(memory_space=pl.ANY),
            pl.BlockSpec((1, D_MODEL, tf), lambda i, j, te, tv, sr: (te[i], 0, j)),
            pl.BlockSpec((1, D_MODEL, tf), lambda i, j, te, tv, sr: (te[i], 0, j)),
            pl.BlockSpec((1, tf, D_MODEL), lambda i, j, te, tv, sr: (te[i], j, 0)),
        ],
        out_specs=pl.BlockSpec((tm, D_MODEL), lambda i, j, te, tv, sr: (i, 0)),
        scratch_shapes=[pltpu.VMEM((2, tm, D_MODEL), F32), pltpu.VMEM((tm, D_MODEL), BF16),
                        pltpu.SemaphoreType.DMA((2,))],
    )
    return pl.pallas_call(
        functools.partial(_moe_ffn_kernel, tm=tm),
        grid_spec=grid_spec,
        out_shape=jax.ShapeDtypeStruct((n, D_MODEL), F32),
        compiler_params=_cparams(("arbitrary", "arbitrary")),
        name="moe_ffn",
    )(tile_expert, tile_valid, src, h, prm["moe_wg"], prm["moe_wu"], prm["moe_wd"])


def _moe_combine_kernel(d0_ref, d1_ref, ys_ref, x_ref, route_ref, o_ref, b0_ref, b1_ref, sem,
                        *, rows):
    base = pl.program_id(0) * rows

    def issue(g, carry):
        for u in range(DMA_UNROLL):
            r = g * DMA_UNROLL + u
            pltpu.make_async_copy(ys_ref.at[pl.ds(d0_ref[base + r], 1)], b0_ref.at[pl.ds(r, 1)],
                                  sem.at[0]).start(priority=0)
            pltpu.make_async_copy(ys_ref.at[pl.ds(d1_ref[base + r], 1)], b1_ref.at[pl.ds(r, 1)],
                                  sem.at[1]).start(priority=1)
        return carry

    lax.fori_loop(0, rows // DMA_UNROLL, issue, 0)
    pltpu.make_async_copy(ys_ref.at[pl.ds(0, rows)], b0_ref, sem.at[0]).wait()
    pltpu.make_async_copy(ys_ref.at[pl.ds(0, rows)], b1_ref, sem.at[1]).wait()
    route = route_ref[...]
    y = route[:, 2:3] * b0_ref[...] + route[:, 3:4] * b1_ref[...]
    o_ref[...] = x_ref[...] + y


def _moe_combine(dest0, dest1, ys, x, route, rows):
    t = x.shape[0]
    grid_spec = pltpu.PrefetchScalarGridSpec(
        num_scalar_prefetch=2,
        grid=(t // rows,),
        in_specs=[
            pl.BlockSpec(memory_space=pl.ANY),
            pl.BlockSpec((rows, D_MODEL), lambda i, a, b: (i, 0)),
            pl.BlockSpec((rows, LANES), lambda i, a, b: (i, 0)),
        ],
        out_specs=pl.BlockSpec((rows, D_MODEL), lambda i, a, b: (i, 0)),
        scratch_shapes=[pltpu.VMEM((rows, D_MODEL), F32), pltpu.VMEM((rows, D_MODEL), F32),
                        pltpu.SemaphoreType.DMA((2,))],
    )
    return pl.pallas_call(
        functools.partial(_moe_combine_kernel, rows=rows),
        grid_spec=grid_spec,
        out_shape=jax.ShapeDtypeStruct((t, D_MODEL), F32),
        compiler_params=_cparams(("arbitrary",)),
        name="moe_combine",
    )(dest0, dest1, ys, x, route)


def _moe(h, x, route, prm, tm, rows):
    t = h.shape[0]
    n_pairs = t * TOP_K
    n_tiles = n_pairs // tm + N_EXPERTS
    expert = route[:, 0:TOP_K].astype(jnp.int32).reshape(n_pairs)
    onehot = (expert[:, None] == jnp.arange(N_EXPERTS, dtype=jnp.int32)[None, :]).astype(jnp.int32)
    csum = jnp.cumsum(onehot, axis=0)
    rank = jnp.sum(csum * onehot, axis=1) - 1
    counts = csum[-1]
    padded = ((counts + tm - 1) // tm) * tm
    ends = jnp.cumsum(padded)
    starts = ends - padded
    dest = jnp.sum(onehot * starts[None, :], axis=1) + rank
    token = jnp.arange(n_pairs, dtype=jnp.int32) // TOP_K
    src = jnp.zeros((n_tiles * tm,), jnp.int32).at[dest].set(token)
    tile_start = jnp.arange(n_tiles, dtype=jnp.int32) * tm
    tile_expert = jnp.minimum(
        jnp.sum((tile_start[:, None] >= ends[None, :]).astype(jnp.int32), axis=1), N_EXPERTS - 1)
    tile_valid = (tile_start < ends[-1]).astype(jnp.int32)
    ys = _moe_ffn(h, src, tile_expert, tile_valid, prm, tm)
    dest2 = dest.reshape(t, TOP_K)
    return _moe_combine(dest2[:, 0], dest2[:, 1], ys, x, route, rows)


def _rope_tables(pos):
    half = HEAD_DIM // 2
    inv = ROPE_THETA ** (-jnp.arange(half, dtype=F32) / half)
    ang = pos.astype(F32)[:, None] * inv[None, :]
    cos = jnp.cos(ang)
    sin = jnp.sin(ang)
    cos_h = jnp.concatenate([cos, cos], axis=1)
    sin_h = jnp.concatenate([-sin, sin], axis=1)
    return jnp.tile(cos_h, (1, LANES // HEAD_DIM)), jnp.tile(sin_h, (1, LANES // HEAD_DIM))


def _rope_tables_t(pos):
    half = HEAD_DIM // 2
    inv = ROPE_THETA ** (-jnp.arange(half, dtype=F32) / half)
    ang = pos.astype(F32)[:, None] * inv[None, :]
    return jnp.cos(ang).T, jnp.sin(ang).T


def _layer_params(l, p):
    idx = jnp.arange(512)
    tril = jnp.tril(jnp.ones((CHUNK, CHUNK), F32))
    ws = p["gmlp_ws"][l] * tril[None]
    pool_bd = jnp.zeros((P_WIDTH, P_WIDTH), F32)
    for g in range(len(POOL_WINDOWS)):
        pool_bd = pool_bd.at[g * P_GC:(g + 1) * P_GC, g * P_GC:(g + 1) * P_GC].set(p["pool_w"][l, g])
    prm = {
        "g_mix": p["norm_mix_g"][l][None, :],
        "w_in": p["w_in"][l].astype(BF16),
        "w_qkv_t": p["w_in"][l][:, 0:3 * 512].T.astype(BF16),
        "w_rest": p["w_in"][l][:, 1024:].astype(BF16),
        "qg_col": jnp.tile(p["q_norm_g"][l], A_QK_HEADS)[:, None],
        "kg_col": jnp.tile(p["k_norm_g"][l], A_QK_HEADS)[:, None],
        "subg_col": p["subln_g"][l][:, None],
        "head_ones": ((idx[:, None] // HEAD_DIM) == (idx[None, :] // HEAD_DIM)).astype(BF16),
        "qg": jnp.tile(p["q_norm_g"][l], A_QK_HEADS)[None, :],
        "kg": jnp.tile(p["k_norm_g"][l], A_QK_HEADS)[None, :],
        "gn_g": p["gmlp_norm_g"][l][None, :],
        "gn_b": p["gmlp_norm_b"][l][None, :],
        "gw_cat": jnp.concatenate([ws[g] for g in range(G_HEADS)], axis=1).astype(BF16),
        "gb_full": jnp.repeat(p["gmlp_bs"][l].T, HEAD_DIM, axis=1),
        "gw0": jnp.repeat(p["gmlp_ws"][l][:, 0, 0], HEAD_DIM)[None, :],
        "gb0": jnp.repeat(p["gmlp_bs"][l][:, 0], HEAD_DIM)[None, :],
        "pool_bd": pool_bd.astype(BF16),
        "pool_scale": p["pool_scale"][l][None, :],
        "lq1": p["lam_q1"][l][None, :], "lk1": p["lam_k1"][l][None, :],
        "lq2": p["lam_q2"][l][None, :], "lk2": p["lam_k2"][l][None, :],
        "subg": p["subln_g"][l][None, :],
        "wo_a": p["w_out"][l][:A_WIDTH].astype(BF16),
        "wo_gp": p["w_out"][l][A_WIDTH:].astype(BF16),
        "g_ffn": p["norm_ffn_g"][l][None, :],
    }
    j = l // 2
    if l % 2 == 0:
        prm["ffn_wg"] = p["ffn_w_gate"][j].astype(BF16)
        prm["ffn_wu"] = p["ffn_w_up"][j].astype(BF16)
        prm["ffn_wd"] = p["ffn_w_down"][j].astype(BF16)
    else:
        prm["router"] = jnp.pad(p["router_w"][j], ((0, 0), (0, LANES - N_EXPERTS))).astype(BF16)
        prm["moe_wg"] = p["moe_w_gate"][j].astype(BF16)
        prm["moe_wu"] = p["moe_w_up"][j].astype(BF16)
        prm["moe_wd"] = p["moe_w_down"][j].astype(BF16)
    return prm


def kernel(x_prompt, x_sample, cache_k, cache_v, state_pool, page_table, norm_mix_g, w_in, q_norm_g, k_norm_g, lam_q1, lam_k1, lam_q2, lam_k2, subln_g, gmlp_norm_g, gmlp_norm_b, gmlp_ws, gmlp_bs, pool_w, pool_scale, w_out, norm_ffn_g, ffn_w_gate, ffn_w_up, ffn_w_down, router_w, moe_w_gate, moe_w_up, moe_w_down):
    params = dict(norm_mix_g=norm_mix_g, w_in=w_in, q_norm_g=q_norm_g, k_norm_g=k_norm_g,
                  lam_q1=lam_q1, lam_k1=lam_k1, lam_q2=lam_q2, lam_k2=lam_k2, subln_g=subln_g,
                  gmlp_norm_g=gmlp_norm_g, gmlp_norm_b=gmlp_norm_b, gmlp_ws=gmlp_ws,
                  gmlp_bs=gmlp_bs, pool_w=pool_w, pool_scale=pool_scale, w_out=w_out,
                  norm_ffn_g=norm_ffn_g, ffn_w_gate=ffn_w_gate, ffn_w_up=ffn_w_up,
                  ffn_w_down=ffn_w_down, router_w=router_w, moe_w_gate=moe_w_gate,
                  moe_w_up=moe_w_up, moe_w_down=moe_w_down)
    batch, seq_len, _ = x_prompt.shape
    n_dec, dec_seq, _ = x_sample.shape
    assert dec_seq == 1
    depth, n_pool = cache_k.shape[0], cache_k.shape[1]
    past_len = page_table.shape[1] * PAGE_SIZE
    t = batch * seq_len

    cos_pt, sin_pt = _rope_tables_t(jnp.arange(seq_len))
    cos_s, sin_s = _rope_tables(jnp.full((n_dec,), past_len, jnp.int32))
    cache_kt = cache_k.transpose(0, 1, 3, 4, 2).reshape(depth, n_pool, 512, PAGE_SIZE)
    cache_vr = cache_v.reshape(depth, n_pool, PAGE_SIZE * A_HEADS, LANES)

    xp = x_prompt.reshape(t, D_MODEL)
    xs = x_sample.reshape(n_dec, D_MODEL)
    kp_l, vp_l, pp_l, ks_l, vs_l, ps_l, gs_l = [], [], [], [], [], [], []
    for l in range(depth):
        prm = _layer_params(l, params)
        routed = l % 2 == 1

        qt, kt, vt, v, gp, px_last = _mix_in_prompt(xp, prm, cos_pt, sin_pt, seq_len)
        a = _attn_prompt(qt, kt, vt, prm, l)
        if routed:
            xp1, hp, route_p = _mix_out(a, gp, xp, prm, TM_PROMPT, True)
        else:
            xp1, hp = _mix_out(a, gp, xp, prm, TM_PROMPT, False)
        kp_l.append(kt.reshape(batch, A_QK_HEADS, HEAD_DIM, seq_len).transpose(0, 3, 1, 2))
        vp_l.append(v.reshape(batch, seq_len, A_HEADS, 2 * HEAD_DIM))
        pp_l.append(px_last.reshape(batch, TM_PROMPT, P_WIDTH)[:, TM_PROMPT - POOL_HIST:, :])

        hist = state_pool[l].transpose(1, 0, 2)
        q_s, k_s, v_s, gp_s, px_s, gvn_s = _mix_in_sample(xs, prm, cos_s, sin_s, hist)
        a_s = _attn_decode(q_s, k_s, v_s, page_table, cache_kt, cache_vr, prm, l)
        if routed:
            xs1, hs, route_s = _mix_out(a_s, gp_s, xs, prm, n_dec, True)
        else:
            xs1, hs = _mix_out(a_s, gp_s, xs, prm, n_dec, False)
        ks_l.append(k_s.reshape(n_dec, 1, A_QK_HEADS, HEAD_DIM))
        vs_l.append(v_s.reshape(n_dec, 1, A_HEADS, 2 * HEAD_DIM))
        ps_l.append(jnp.concatenate([state_pool[l][:, 1:, :], px_s[:, None, :]], axis=1))
        gs_l.append(gvn_s.reshape(n_dec, 1, G_HEADS, HEAD_DIM))

        if routed:
            xp = _moe(hp, xp1, route_p, prm, TM_MOE, 256)
            xs = _moe(hs, xs1, route_s, prm, n_dec, n_dec)
        else:
            xp = _ffn(hp, xp1, prm, TM_PROMPT)
            xs = _ffn(hs, xs1, prm, n_dec)

    return (xp.reshape(batch, seq_len, D_MODEL), xs.reshape(n_dec, 1, D_MODEL),
            jnp.stack(kp_l), jnp.stack(vp_l), jnp.stack(pp_l), jnp.stack(ks_l), jnp.stack(vs_l),
            jnp.stack(ps_l), jnp.stack(gs_l))
```

```python
import functools
import math

import jax
import jax.numpy as jnp
from jax import lax
from jax.experimental import pallas as pl
from jax.experimental.pallas import tpu as pltpu

F32 = jnp.float32
BF16 = jnp.bfloat16

D_MODEL = 1024
HEAD_DIM = 64
A_HEADS = 4
A_QK_HEADS = 8
A_WIDTH = 512
G_WIDTH = 256
G_HEADS = 4
CHUNK = 128
P_WIDTH = 256
POOL_WINDOWS = (2, 4, 8, 16)
P_GC = 64
POOL_HIST = 15
D_IN = 2304
ROPE_THETA = 10000.0
N_EXPERTS = 8
TOP_K = 2
EPS = 1e-6
PAGE_SIZE = 128

LANES = 128
VMEM_LIMIT = 56 * 1024 * 1024

TM_PROMPT = 512
TQ = 1024
TM_MOE = 512
TF_MOE = 896
PAGES_PER_STEP = 8
DECODE_SLOTS = 4
DMA_UNROLL = 8


def _cparams(sem):
    return pltpu.CompilerParams(dimension_semantics=sem, vmem_limit_bytes=VMEM_LIMIT)


def _lam_init(layer):
    return 0.8 - 0.6 * math.exp(-0.3 * layer)


def _lam_value(lq1, lk1, lq2, lk2, lam_init):
    a = jnp.sum(lq1 * lk1, axis=-1, keepdims=True)
    b = jnp.sum(lq2 * lk2, axis=-1, keepdims=True)
    return jnp.exp(a) - jnp.exp(b) + lam_init


def _swap_halves(x):
    lane = lax.broadcasted_iota(jnp.int32, x.shape, 1)
    first_half = (lane % HEAD_DIM) < (HEAD_DIM // 2)
    return jnp.where(first_half, pltpu.roll(x, LANES - HEAD_DIM // 2, 1),
                     pltpu.roll(x, HEAD_DIM // 2, 1))


def _head_norm_rope(t, head_ones, gain, cos, sin):
    sq = t * t
    ss = jnp.dot(sq.astype(BF16), head_ones, preferred_element_type=F32)
    ss = ss + jnp.dot((sq - sq.astype(BF16).astype(F32)).astype(BF16), head_ones,
                      preferred_element_type=F32)
    tn = t * lax.rsqrt(ss * (1.0 / HEAD_DIM) + EPS) * gain
    cols = []
    for c in range(t.shape[1] // LANES):
        tc = tn[:, c * LANES:(c + 1) * LANES]
        cols.append(tc * cos + _swap_halves(tc) * sin)
    return jnp.concatenate(cols, axis=1)


def _project(x, g_mix, w_in, head_ones, qg, kg, cos, sin, gn_g, gn_b):
    ms = jnp.mean(x * x, axis=-1, keepdims=True)
    h = x * lax.rsqrt(ms + EPS) * g_mix
    z = jnp.dot(h.astype(BF16), w_in, preferred_element_type=F32)
    q = _head_norm_rope(z[:, 0:512], head_ones, qg, cos, sin)
    k = _head_norm_rope(z[:, 512:1024], head_ones, kg, cos, sin)
    v = z[:, 1024:1536]
    gu = z[:, 1536:1792]
    gv = z[:, 1792:2048]
    px = z[:, 2048:2304]
    mu = jnp.mean(gv, axis=-1, keepdims=True)
    gc = gv - mu
    var = jnp.mean(gc * gc, axis=-1, keepdims=True)
    gvn = gc * lax.rsqrt(var + EPS) * gn_g + gn_b
    return q, k, v, gu, gvn, px


def _pool_window_select(s2, s4, s8, s16):
    lane = lax.broadcasted_iota(jnp.int32, s2.shape, 1)
    return jnp.where(lane < P_GC, s2,
                     jnp.where(lane < 2 * P_GC, s4, jnp.where(lane < 3 * P_GC, s8, s16)))


def _pool_window_sizes(shape):
    lane = lax.broadcasted_iota(jnp.int32, shape, 1)
    return jnp.where(lane < P_GC, 2.0,
                     jnp.where(lane < 2 * P_GC, 4.0,
                               jnp.where(lane < 3 * P_GC, 8.0, 16.0))).astype(F32)


def _head_norm_rope_t(t, gain, cos, sin):
    n = t.shape[1]
    half = HEAD_DIM // 2
    th = t.reshape(A_QK_HEADS, HEAD_DIM, n)
    ms = jnp.mean(th * th, axis=1, keepdims=True)
    tn = (th * lax.rsqrt(ms + EPS)).reshape(A_QK_HEADS * HEAD_DIM, n) * gain
    tn = tn.reshape(A_QK_HEADS, HEAD_DIM, n)
    x1 = tn[:, 0:half, :]
    x2 = tn[:, half:HEAD_DIM, :]
    out = jnp.concatenate([x1 * cos[None] - x2 * sin[None], x2 * cos[None] + x1 * sin[None]],
                          axis=1)
    return out.reshape(A_QK_HEADS * HEAD_DIM, n)


def _mix_in_prompt_kernel(x_ref, gmix_ref, wqkvt_ref, wrest_ref, qg_ref, kg_ref, cos_ref, sin_ref,
                          gng_ref, gnb_ref, gw_ref, gb_ref, pw_ref, ps_ref,
                          qt_ref, kt_ref, vt_ref, v_ref, gp_ref, pxl_ref, ext_ref,
                          *, tm, tiles_per_seq):
    i = pl.program_id(0)
    x = x_ref[...]
    ms = jnp.mean(x * x, axis=-1, keepdims=True)
    h = (x * lax.rsqrt(ms + EPS) * gmix_ref[...]).astype(BF16)
    zt = lax.dot_general(wqkvt_ref[...], h, (((1,), (1,)), ((), ())), preferred_element_type=F32)
    z = jnp.dot(h, wrest_ref[...], preferred_element_type=F32)
    cos = cos_ref[...]
    sin = sin_ref[...]
    qt = _head_norm_rope_t(zt[0:512, :], qg_ref[...], cos, sin)
    qt_ref[0] = (qt * (HEAD_DIM ** -0.5)).astype(BF16)
    kt_ref[0] = _head_norm_rope_t(zt[512:1024, :], kg_ref[...], cos, sin)
    vt_ref[0] = zt[1024:1536, :].astype(BF16)
    v_ref[...] = z[:, 0:512]
    gu = z[:, 512:768]
    gv = z[:, 768:1024]
    px = z[:, 1024:1280]
    mu = jnp.mean(gv, axis=-1, keepdims=True)
    gc = gv - mu
    var = jnp.mean(gc * gc, axis=-1, keepdims=True)
    gvn = gc * lax.rsqrt(var + EPS) * gng_ref[...] + gnb_ref[...]
    pxl_ref[...] = px

    lane = lax.broadcasted_iota(jnp.int32, (CHUNK, G_WIDTH), 1)
    for c in range(tm // CHUNK):
        vn_c = gvn[c * CHUNK:(c + 1) * CHUNK, :]
        rhs = jnp.concatenate(
            [jnp.where((lane // HEAD_DIM) == g, vn_c, 0.0).astype(BF16) for g in range(G_HEADS)],
            axis=0)
        mixed = jnp.dot(gw_ref[...], rhs, preferred_element_type=F32) + gb_ref[...]
        gp_ref[c * CHUNK:(c + 1) * CHUNK, 0:G_WIDTH] = (
            gu[c * CHUNK:(c + 1) * CHUNK, :] * mixed).astype(BF16)

    seq_tile = i % tiles_per_seq

    @pl.when(seq_tile == 0)
    def _():
        ext_ref[0:16, :] = jnp.zeros((16, P_WIDTH), F32)

    @pl.when(seq_tile != 0)
    def _():
        ext_ref[0:16, :] = ext_ref[tm:tm + 16, :]

    ext_ref[16:16 + tm, :] = px

    def back(kk):
        return ext_ref[16 - kk:16 - kk + tm, :]

    s2 = px + back(1)
    s4 = s2 + back(2) + back(3)
    s8 = s4 + back(4) + back(5) + back(6) + back(7)
    s16 = s8
    for kk in range(8, 16):
        s16 = s16 + back(kk)
    tot = _pool_window_select(s2, s4, s8, s16)
    pos = (seq_tile * tm + lax.broadcasted_iota(jnp.int32, (tm, P_WIDTH), 0)).astype(F32)
    cnt = jnp.minimum(_pool_window_sizes((tm, P_WIDTH)), pos + 1.0)
    d = tot / cnt - px
    p = jnp.dot(d.astype(BF16), pw_ref[...], preferred_element_type=F32) * ps_ref[...]
    gp_ref[:, G_WIDTH:G_WIDTH + P_WIDTH] = p.astype(BF16)


def _mix_in_sample_kernel(x_ref, gmix_ref, win_ref, ones_ref, qg_ref, kg_ref, cos_ref, sin_ref,
                          gng_ref, gnb_ref, gw0_ref, gb0_ref, pw_ref, ps_ref, hist_ref,
                          q_ref, k_ref, v_ref, gp_ref, px_ref, gvn_ref):
    q, k, v, gu, gvn, px = _project(
        x_ref[...], gmix_ref[...], win_ref[...], ones_ref[...], qg_ref[...], kg_ref[...],
        cos_ref[...], sin_ref[...], gng_ref[...], gnb_ref[...])
    q_ref[...] = q * (HEAD_DIM ** -0.5)
    k_ref[...] = k
    v_ref[...] = v
    px_ref[...] = px
    gvn_ref[...] = gvn
    mixed = gw0_ref[...] * gvn + gb0_ref[...]
    gp_ref[:, 0:G_WIDTH] = (gu * mixed).astype(BF16)

    def tail(n):
        acc = px
        for r in range(POOL_HIST - n, POOL_HIST):
            acc = acc + hist_ref[r]
        return acc

    tot = _pool_window_select(tail(1), tail(3), tail(7), tail(15))
    d = tot / _pool_window_sizes(px.shape) - px
    p = jnp.dot(d.astype(BF16), pw_ref[...], preferred_element_type=F32) * ps_ref[...]
    gp_ref[:, G_WIDTH:G_WIDTH + P_WIDTH] = p.astype(BF16)


def _const_spec(shape):
    return pl.BlockSpec(shape, lambda *_: (0,) * len(shape))


def _mix_in_prompt(x, prm, cos, sin, seq_len):
    t = x.shape[0]
    tm = TM_PROMPT
    tps = seq_len // tm
    nseq = t // seq_len
    kern = functools.partial(_mix_in_prompt_kernel, tm=tm, tiles_per_seq=tps)
    row = lambda w: pl.BlockSpec((tm, w), lambda i: (i, 0))
    seq_t = pl.BlockSpec((1, 512, tm), lambda i: (i // tps, 0, i % tps))
    return pl.pallas_call(
        kern,
        grid=(t // tm,),
        in_specs=[
            row(D_MODEL), _const_spec((1, D_MODEL)), _const_spec((3 * 512, D_MODEL)),
            _const_spec((D_MODEL, D_IN - 1024)), _const_spec((512, 1)), _const_spec((512, 1)),
            pl.BlockSpec((HEAD_DIM // 2, tm), lambda i: (0, i % tps)),
            pl.BlockSpec((HEAD_DIM // 2, tm), lambda i: (0, i % tps)),
            _const_spec((1, G_WIDTH)), _const_spec((1, G_WIDTH)),
            _const_spec((CHUNK, G_HEADS * CHUNK)), _const_spec((CHUNK, G_WIDTH)),
            _const_spec((P_WIDTH, P_WIDTH)), _const_spec((1, P_WIDTH)),
        ],
        out_specs=[
            seq_t, seq_t, seq_t, row(512), row(512),
            pl.BlockSpec((tm, P_WIDTH), lambda i: (i // tps, 0)),
        ],
        out_shape=[
            jax.ShapeDtypeStruct((nseq, 512, seq_len), BF16),
            jax.ShapeDtypeStruct((nseq, 512, seq_len), F32),
            jax.ShapeDtypeStruct((nseq, 512, seq_len), BF16),
            jax.ShapeDtypeStruct((t, 512), F32),
            jax.ShapeDtypeStruct((t, 512), BF16),
            jax.ShapeDtypeStruct((nseq * tm, P_WIDTH), F32),
        ],
        scratch_shapes=[pltpu.VMEM((tm + 16, P_WIDTH), F32)],
        compiler_params=_cparams(("arbitrary",)),
        name="mix_in_prompt",
    )(x, prm["g_mix"], prm["w_qkv_t"], prm["w_rest"], prm["qg_col"], prm["kg_col"], cos, sin,
      prm["gn_g"], prm["gn_b"], prm["gw_cat"], prm["gb_full"], prm["pool_bd"], prm["pool_scale"])


def _mix_in_sample(x, prm, cos, sin, hist):
    n = x.shape[0]
    full = lambda *s: _const_spec(s)
    return pl.pallas_call(
        _mix_in_sample_kernel,
        grid=(1,),
        in_specs=[
            full(n, D_MODEL), full(1, D_MODEL), full(D_MODEL, D_IN), full(512, 512),
            full(1, 512), full(1, 512), full(n, LANES), full(n, LANES),
            full(1, G_WIDTH), full(1, G_WIDTH), full(1, G_WIDTH), full(1, G_WIDTH),
            full(P_WIDTH, P_WIDTH), full(1, P_WIDTH), full(POOL_HIST, n, P_WIDTH),
        ],
        out_specs=[full(n, 512), full(n, 512), full(n, 512), full(n, 512),
                   full(n, P_WIDTH), full(n, G_WIDTH)],
        out_shape=[
            jax.ShapeDtypeStruct((n, 512), F32),
            jax.ShapeDtypeStruct((n, 512), F32),
            jax.ShapeDtypeStruct((n, 512), F32),
            jax.ShapeDtypeStruct((n, 512), BF16),
            jax.ShapeDtypeStruct((n, P_WIDTH), F32),
            jax.ShapeDtypeStruct((n, G_WIDTH), F32),
        ],
        compiler_params=_cparams(("arbitrary",)),
        name="mix_in_sample",
    )(x, prm["g_mix"], prm["w_in"], prm["head_ones"], prm["qg"], prm["kg"], cos, sin,
      prm["gn_g"], prm["gn_b"], prm["gw0"], prm["gb0"], prm["pool_bd"], prm["pool_scale"], hist)


def _subln(o, subg, lam_init):
    ms = jnp.mean(o * o, axis=-1, keepdims=True)
    return o * lax.rsqrt(ms + EPS) * subg * (1.0 - lam_init)


def _attn_prompt_kernel(qi_tab, ki_tab, qt_ref, kt_ref, vt_ref, lq1_ref, lk1_ref, lq2_ref,
                        lk2_ref, subg_ref, o_ref, qs_ref, m_ref, l_ref, acc_ref, *, tq, lam_init):
    step = pl.program_id(2)
    qi = qi_tab[step]
    ki = ki_tab[step]

    @pl.when(ki == 0)
    def _():
        q = qt_ref[0]
        row = lax.broadcasted_iota(jnp.int32, q.shape, 0)
        zero = jnp.zeros_like(q)
        qs_ref[:, 0:tq] = jnp.where(row < HEAD_DIM, q, zero)
        qs_ref[:, tq:2 * tq] = jnp.where(row >= HEAD_DIM, q, zero)
        m_ref[...] = jnp.full(m_ref.shape, -jnp.inf, F32)
        l_ref[...] = jnp.zeros(l_ref.shape, F32)
        acc_ref[...] = jnp.zeros(acc_ref.shape, F32)

    def update(masked):
        s = lax.dot_general(kt_ref[0].astype(BF16), qs_ref[...], (((0,), (0,)), ((), ())),
                            preferred_element_type=F32)
        if masked:
            r = lax.broadcasted_iota(jnp.int32, s.shape, 0)
            c = lax.broadcasted_iota(jnp.int32, s.shape, 1) % tq
            s = jnp.where(r <= c, s, -jnp.inf)
        m_prev = m_ref[...]
        m_new = jnp.maximum(m_prev, jnp.max(s, axis=0, keepdims=True))
        alpha = jnp.exp(m_prev - m_new)
        p = jnp.exp(s - m_new[0:1, :])
        l_ref[...] = alpha * l_ref[...] + jnp.sum(p, axis=0, keepdims=True)
        acc_ref[...] = alpha[0:1, :] * acc_ref[...] + jnp.dot(
            vt_ref[0], p.astype(BF16), preferred_element_type=F32)
        m_ref[...] = m_new

    @pl.when(ki < qi)
    def _():
        update(False)

    @pl.when(ki == qi)
    def _():
        update(True)
        o = acc_ref[...] / l_ref[0:1, :]
        lam = _lam_value(lq1_ref[...], lk1_ref[...], lq2_ref[...], lk2_ref[...], lam_init)
        od = o[:, 0:tq] - lam * o[:, tq:2 * tq]
        ms = jnp.mean(od * od, axis=0, keepdims=True)
        o_ref[0] = (od * lax.rsqrt(ms + EPS) * subg_ref[...] * (1.0 - lam_init)).astype(BF16)


def _attn_prompt(qt, kt, vt, prm, layer):
    nseq, _, seq_len = qt.shape
    tq = TQ
    nq = seq_len // tq
    pairs = [(a, b) for a in range(nq) for b in range(a + 1)]
    qi_tab = jnp.asarray([p[0] for p in pairs], jnp.int32)
    ki_tab = jnp.asarray([p[1] for p in pairs], jnp.int32)
    kern = functools.partial(_attn_prompt_kernel, tq=tq, lam_init=_lam_init(layer))
    vec = lambda w: pl.BlockSpec((1, w), lambda b, h, s, qt_, kt_: (0, 0))
    q_tile = pl.BlockSpec((1, LANES, tq), lambda b, h, s, qt_, kt_: (b, h, qt_[s]))
    k_tile = pl.BlockSpec((1, LANES, tq), lambda b, h, s, qt_, kt_: (b, h, kt_[s]))
    grid_spec = pltpu.PrefetchScalarGridSpec(
        num_scalar_prefetch=2,
        grid=(nseq, A_HEADS, len(pairs)),
        in_specs=[
            q_tile, k_tile, k_tile,
            vec(HEAD_DIM), vec(HEAD_DIM), vec(HEAD_DIM), vec(HEAD_DIM),
            pl.BlockSpec((LANES, 1), lambda b, h, s, qt_, kt_: (0, 0)),
        ],
        out_specs=q_tile,
        scratch_shapes=[
            pltpu.VMEM((LANES, 2 * tq), BF16),
            pltpu.VMEM((8, 2 * tq), F32),
            pltpu.VMEM((8, 2 * tq), F32),
            pltpu.VMEM((LANES, 2 * tq), F32),
        ],
    )
    return pl.pallas_call(
        kern,
        grid_spec=grid_spec,
        out_shape=jax.ShapeDtypeStruct((nseq, 512, seq_len), BF16),
        compiler_params=_cparams(("arbitrary", "arbitrary", "arbitrary")),
        name="attn_prompt",
    )(qi_tab, ki_tab, qt, kt, vt, prm["lq1"], prm["lk1"], prm["lq2"], prm["lk2"],
      prm["subg_col"])


def _head_scores(qb, kt):
    prod = qb * kt
    return jnp.sum(prod.reshape(A_QK_HEADS, HEAD_DIM, LANES), axis=1)


def _attn_decode_kernel(pt_ref, qb_ref, knb_ref, vn_ref, lq1_ref, lk1_ref, lq2_ref, lk2_ref,
                        subg_ref, ck_ref, cv_ref, o_ref, buf_ref, sc_ref, sem,
                        *, npg, layer, lam_init):
    b = pl.program_id(0)
    n_seq = pl.num_programs(0)
    n_groups = pt_ref.shape[1] // npg

    def bf16_round(x):
        return x.astype(BF16).astype(F32)

    qb = bf16_round(qb_ref[0])

    def start_group(cache_ref, seq, grp, slot):
        for i in range(npg):
            page = pt_ref[seq, grp * npg + i]
            pltpu.make_async_copy(cache_ref.at[layer, page], buf_ref.at[slot, i],
                                  sem.at[slot]).start()

    def wait_group(slot):
        pltpu.make_async_copy(ck_ref.at[layer, pl.ds(0, npg)], buf_ref.at[slot],
                              sem.at[slot]).wait()

    n_items = 2 * n_groups
    lookahead = DECODE_SLOTS - 1

    def start_item(seq, item):
        slot = item % DECODE_SLOTS

        @pl.when(item < n_groups)
        def _():
            start_group(ck_ref, seq, item, slot)

        @pl.when(item >= n_groups)
        def _():
            start_group(cv_ref, seq, item - n_groups, slot)

    def prefetch(item):
        nxt = item + lookahead

        @pl.when(nxt < n_items)
        def _():
            start_item(b, nxt)

        @pl.when((nxt >= n_items) & (b + 1 < n_seq))
        def _():
            start_item(jnp.minimum(b + 1, n_seq - 1), nxt - n_items)

    @pl.when(b == 0)
    def _():
        for item in range(lookahead):
            start_group(ck_ref, 0, item, item % DECODE_SLOTS)

    def key_pass(grp, m_run):
        slot = grp % DECODE_SLOTS
        prefetch(grp)
        wait_group(slot)
        sc = jnp.concatenate(
            [_head_scores(qb, bf16_round(buf_ref[slot, i])) for i in range(npg)], axis=1)
        sc_ref[grp] = sc
        return jnp.maximum(m_run, jnp.max(sc, axis=1, keepdims=True))

    s_new = _head_scores(qb_ref[0], knb_ref[0])
    m_fin = lax.fori_loop(0, n_groups, key_pass, s_new)

    def exp_pass(grp, l_run):
        p = jnp.exp(sc_ref[grp] - m_fin[:, 0:1])
        sc_ref[grp] = p
        return l_run + jnp.sum(p, axis=1, keepdims=True)

    p_new = jnp.exp(s_new - m_fin)
    l_fin = lax.fori_loop(0, n_groups, exp_pass, jnp.zeros((A_QK_HEADS, LANES), F32)) + p_new

    lam = _lam_value(lq1_ref[...], lk1_ref[...], lq2_ref[...], lk2_ref[...], lam_init)
    row8 = lax.broadcasted_iota(jnp.int32, (A_QK_HEADS, LANES), 0)
    coef = jnp.where(row8 % 2 == 0, 1.0, -lam)

    def diff_weights(p):
        n = p.shape[1] // LANES
        wc = p / jnp.tile(l_fin, (1, n)) * jnp.tile(coef, (1, n))
        wd = wc + pltpu.roll(wc, A_QK_HEADS - 1, 0)
        even = lax.broadcasted_iota(jnp.int32, wd.shape, 0) % 2 == 0
        return jnp.where(even, wd, 0.0)

    head_of_row = lax.broadcasted_iota(jnp.int32, (A_QK_HEADS, A_HEADS * LANES), 0) // 2
    head_of_col = lax.broadcasted_iota(jnp.int32, (A_QK_HEADS, A_HEADS * LANES), 1) // LANES

    def value_pass(grp, acc):
        slot = (n_groups + grp) % DECODE_SLOTS
        prefetch(n_groups + grp)
        wait_group(slot)
        w = diff_weights(sc_ref[grp])
        p_parts, v_parts = [], []
        for i in range(npg):
            wi = w[:, i * LANES:(i + 1) * LANES]
            p_parts.append(jnp.where(head_of_row == head_of_col,
                                     jnp.concatenate([wi] * A_HEADS, axis=1), 0.0).astype(BF16))
            for h in range(A_HEADS):
                v_parts.append(
                    buf_ref[slot, i, pl.ds(h, PAGE_SIZE, stride=A_HEADS), :].astype(BF16))
        return acc + jnp.dot(jnp.concatenate(p_parts, axis=1), jnp.concatenate(v_parts, axis=0),
                             preferred_element_type=F32)

    acc = lax.fori_loop(0, n_groups, value_pass, jnp.zeros((A_QK_HEADS, LANES), F32))
    acc = acc + diff_weights(p_new) * vn_ref[0]
    o_ref[0] = _subln(acc, subg_ref[...], lam_init)


def _attn_decode(q_s, k_s, v_s, page_table, cache_kt, cache_vr, prm, layer):
    n = q_s.shape[0]
    npg = PAGES_PER_STEP
    assert page_table.shape[1] % npg == 0
    n_groups = page_table.shape[1] // npg
    assert (2 * n_groups) % DECODE_SLOTS == 0 and DECODE_SLOTS - 1 <= n_groups
    qb = jnp.broadcast_to(q_s[:, :, None], (n, 512, LANES))
    knb = jnp.broadcast_to(k_s[:, :, None], (n, 512, LANES))
    vn8 = jnp.repeat(v_s.reshape(n, A_HEADS, LANES), 2, axis=1)
    kern = functools.partial(_attn_decode_kernel, npg=npg, layer=layer,
                             lam_init=_lam_init(layer))
    vec = lambda w: pl.BlockSpec((1, w), lambda b, pt: (0, 0))
    seq3 = lambda r: pl.BlockSpec((1, r, LANES), lambda b, pt: (b, 0, 0))
    grid_spec = pltpu.PrefetchScalarGridSpec(
        num_scalar_prefetch=1,
        grid=(n,),
        in_specs=[seq3(512), seq3(512), seq3(A_QK_HEADS),
                  vec(HEAD_DIM), vec(HEAD_DIM), vec(HEAD_DIM), vec(HEAD_DIM), vec(LANES),
                  pl.BlockSpec(memory_space=pl.ANY), pl.BlockSpec(memory_space=pl.ANY)],
        out_specs=seq3(A_QK_HEADS),
        scratch_shapes=[pltpu.VMEM((DECODE_SLOTS, npg, PAGE_SIZE * A_HEADS, LANES), F32),
                        pltpu.VMEM((n_groups, A_QK_HEADS, npg * PAGE_SIZE), F32),
                        pltpu.SemaphoreType.DMA((DECODE_SLOTS,))],
    )
    out = pl.pallas_call(
        kern,
        grid_spec=grid_spec,
        out_shape=jax.ShapeDtypeStruct((n, A_QK_HEADS, LANES), F32),
        compiler_params=_cparams(("arbitrary",)),
        name="attn_decode",
    )(page_table, qb, knb, vn8, prm["lq1"], prm["lk1"], prm["lq2"], prm["lk2"], prm["subg"],
      cache_kt, cache_vr)
    return out[:, 0::2, :].reshape(n, A_WIDTH).astype(BF16)


def _mix_out_kernel(a_ref, gp_ref, x_ref, woa_ref, wogp_ref, g_ref, *rest, routed, a_transposed):
    if routed:
        rw_ref, x1_ref, h_ref, route_ref = rest
    else:
        x1_ref, h_ref = rest
    if a_transposed:
        y = lax.dot_general(a_ref[0], woa_ref[...], (((0,), (0,)), ((), ())),
                            preferred_element_type=F32)
    else:
        y = jnp.dot(a_ref[...], woa_ref[...], preferred_element_type=F32)
    y = y + jnp.dot(gp_ref[...], wogp_ref[...], preferred_element_type=F32)
    x1 = x_ref[...] + y
    x1_ref[...] = x1
    ms = jnp.mean(x1 * x1, axis=-1, keepdims=True)
    h = x1 * lax.rsqrt(ms + EPS) * g_ref[...]
    h_ref[...] = h.astype(h_ref.dtype)
    if routed:
        logits = jnp.dot(h.astype(BF16), rw_ref[...], preferred_element_type=F32)
        lane = lax.broadcasted_iota(jnp.int32, logits.shape, 1)
        lg = jnp.where(lane < N_EXPERTS, logits, -jnp.inf)
        m1 = jnp.max(lg, axis=1, keepdims=True)
        i1 = jnp.min(jnp.where(lg == m1, lane, LANES), axis=1, keepdims=True)
        lg2 = jnp.where(lane == i1, -jnp.inf, lg)
        m2 = jnp.max(lg2, axis=1, keepdims=True)
        i2 = jnp.min(jnp.where(lg2 == m2, lane, LANES), axis=1, keepdims=True)
        e2 = jnp.exp(m2 - m1)
        den = 1.0 + e2
        route = jnp.where(lane == 0, i1.astype(F32),
                          jnp.where(lane == 1, i2.astype(F32),
                                    jnp.where(lane == 2, 1.0 / den,
                                              jnp.where(lane == 3, e2 / den, 0.0))))
        route_ref[...] = route


def _mix_out(a, gp, x, prm, tm, routed):
    t = x.shape[0]
    row = lambda w: pl.BlockSpec((tm, w), lambda i: (i, 0))
    a_transposed = a.ndim == 3
    if a_transposed:
        tps = a.shape[2] // tm
        a_spec = pl.BlockSpec((1, 512, tm), lambda i: (i // tps, 0, i % tps))
    else:
        a_spec = row(512)
    in_specs = [a_spec, row(512), row(D_MODEL), _const_spec((512, D_MODEL)),
                _const_spec((512, D_MODEL)), _const_spec((1, D_MODEL))]
    args = [a, gp, x, prm["wo_a"], prm["wo_gp"], prm["g_ffn"]]
    out_specs = [row(D_MODEL), row(D_MODEL)]
    out_shape = [jax.ShapeDtypeStruct((t, D_MODEL), F32),
                 jax.ShapeDtypeStruct((t, D_MODEL), F32 if routed else BF16)]
    if routed:
        in_specs.append(_const_spec((D_MODEL, LANES)))
        args.append(prm["router"])
        out_specs.append(row(LANES))
        out_shape.append(jax.ShapeDtypeStruct((t, LANES), F32))
    return pl.pallas_call(
        functools.partial(_mix_out_kernel, routed=routed, a_transposed=a_transposed),
        grid=(t // tm,),
        in_specs=in_specs, out_specs=out_specs, out_shape=out_shape,
        compiler_params=_cparams(("arbitrary",)),
        name="mix_out",
    )(*args)


def _silu(x):
    return x * (1.0 / (1.0 + jnp.exp(-x)))


def _ffn_kernel(h_ref, x_ref, wg_ref, wu_ref, wd_ref, o_ref, *, n_chunks):
    h = h_ref[...]
    f = wg_ref.shape[1]
    fc = f // n_chunks
    acc = x_ref[...]
    for c in range(n_chunks):
        gate = jnp.dot(h, wg_ref[:, c * fc:(c + 1) * fc], preferred_element_type=F32)
        up = jnp.dot(h, wu_ref[:, c * fc:(c + 1) * fc], preferred_element_type=F32)
        act = (_silu(gate) * up).astype(BF16)
        acc = acc + jnp.dot(act, wd_ref[c * fc:(c + 1) * fc, :], preferred_element_type=F32)
    o_ref[...] = acc


def _ffn(h, x, prm, tm):
    t = x.shape[0]
    f = prm["ffn_wg"].shape[1]
    row = lambda w: pl.BlockSpec((tm, w), lambda i: (i, 0))
    resident = lambda s: pl.BlockSpec(s, lambda i: (0, 0), pipeline_mode=pl.Buffered(1))
    return pl.pallas_call(
        functools.partial(_ffn_kernel, n_chunks=2),
        grid=(t // tm,),
        in_specs=[row(D_MODEL), row(D_MODEL), resident((D_MODEL, f)), resident((D_MODEL, f)),
                  resident((f, D_MODEL))],
        out_specs=row(D_MODEL),
        out_shape=jax.ShapeDtypeStruct((t, D_MODEL), F32),
        compiler_params=_cparams(("arbitrary",)),
        name="ffn_dense",
    )(h, x, prm["ffn_wg"], prm["ffn_wu"], prm["ffn_wd"])


def _moe_ffn_kernel(te_ref, tv_ref, src_ref, h_ref, wg_ref, wu_ref, wd_ref, o_ref,
                    xg_ref, xb_ref, sem, *, tm):
    i = pl.program_id(0)
    j = pl.program_id(1)
    n_tiles = pl.num_programs(0)
    valid = tv_ref[i] != 0

    def start_gather(tile):
        slot = tile % 2
        base = tile * tm

        def issue(g, carry):
            for u in range(DMA_UNROLL):
                r = g * DMA_UNROLL + u
                pltpu.make_async_copy(h_ref.at[pl.ds(src_ref[base + r], 1)],
                                      xg_ref.at[slot, pl.ds(r, 1)],
                                      sem.at[slot]).start(priority=u % 2)
            return carry

        lax.fori_loop(0, tm // DMA_UNROLL, issue, 0)

    @pl.when((i == 0) & (j == 0) & valid)
    def _():
        start_gather(i)

    @pl.when((j == 0) & valid)
    def _():
        slot = i % 2
        pltpu.make_async_copy(h_ref.at[pl.ds(0, tm)], xg_ref.at[slot], sem.at[slot]).wait()
        xb_ref[...] = xg_ref[slot].astype(BF16)

    @pl.when(j == 0)
    def _():
        o_ref[...] = jnp.zeros(o_ref.shape, F32)

    @pl.when((j == 1) & (i + 1 < n_tiles))
    def _():
        @pl.when(tv_ref[jnp.minimum(i + 1, n_tiles - 1)] != 0)
        def _():
            start_gather(i + 1)

    @pl.when(valid)
    def _():
        xb = xb_ref[...]
        gate = jnp.dot(xb, wg_ref[0], preferred_element_type=F32)
        up = jnp.dot(xb, wu_ref[0], preferred_element_type=F32)
        act = (_silu(gate) * up).astype(BF16)
        o_ref[...] += jnp.dot(act, wd_ref[0], preferred_element_type=F32)


def _moe_ffn(h, src, tile_expert, tile_valid, prm, tm):
    n = src.shape[0]
    f = prm["moe_wg"].shape[2]
    tf = TF_MOE
    assert f // tf >= 2
    grid_spec = pltpu.PrefetchScalarGridSpec(
        num_scalar_prefetch=3,
        grid=(n // tm, f // tf),
        in_specs=[
            pl.BlockSpec(memory_space=pl.ANY),
            pl.BlockSpec((1, D_MODEL, tf), lambda i, j, te, tv, sr: (te[i], 0, j)),
            pl.BlockSpec((1, D_MODEL, tf), lambda i, j, te, tv, sr: (te[i], 0, j)),
            pl.BlockSpec((1, tf, D_MODEL), lambda i, j, te, tv, sr: (te[i], j, 0)),
        ],
        out_specs=pl.BlockSpec((tm, D_MODEL), lambda i, j, te, tv, sr: (i, 0)),
        scratch_shapes=[pltpu.VMEM((2, tm, D_MODEL), F32), pltpu.VMEM((tm, D_MODEL), BF16),
                        pltpu.SemaphoreType.DMA((2,))],
    )
    return pl.pallas_call(
        functools.partial(_moe_ffn_kernel, tm=tm),
        grid_spec=grid_spec,
        out_shape=jax.ShapeDtypeStruct((n, D_MODEL), F32),
        compiler_params=_cparams(("arbitrary", "arbitrary")),
        name="moe_ffn",
    )(tile_expert, tile_valid, src, h, prm["moe_wg"], prm["moe_wu"], prm["moe_wd"])


def _moe_combine_kernel(d0_ref, d1_ref, ys_ref, x_ref, route_ref, o_ref, b0_ref, b1_ref, sem,
                        *, rows):
    base = pl.program_id(0) * rows

    def issue(g, carry):
        for u in range(DMA_UNROLL):
            r = g * DMA_UNROLL + u
            pltpu.make_async_copy(ys_ref.at[pl.ds(d0_ref[base + r], 1)], b0_ref.at[pl.ds(r, 1)],
                                  sem.at[0]).start(priority=0)
            pltpu.make_async_copy(ys_ref.at[pl.ds(d1_ref[base + r], 1)], b1_ref.at[pl.ds(r, 1)],
                                  sem.at[1]).start(priority=1)
        return carry

    lax.fori_loop(0, rows // DMA_UNROLL, issue, 0)
    pltpu.make_async_copy(ys_ref.at[pl.ds(0, rows)], b0_ref, sem.at[0]).wait()
    pltpu.make_async_copy(ys_ref.at[pl.ds(0, rows)], b1_ref, sem.at[1]).wait()
    route = route_ref[...]
    y = route[:, 2:3] * b0_ref[...] + route[:, 3:4] * b1_ref[...]
    o_ref[...] = x_ref[...] + y


def _moe_combine(dest0, dest1, ys, x, route, rows):
    t = x.shape[0]
    grid_spec = pltpu.PrefetchScalarGridSpec(
        num_scalar_prefetch=2,
        grid=(t // rows,),
        in_specs=[
            pl.BlockSpec(memory_space=pl.ANY),
            pl.BlockSpec((rows, D_MODEL), lambda i, a, b: (i, 0)),
            pl.BlockSpec((rows, LANES), lambda i, a, b: (i, 0)),
        ],
        out_specs=pl.BlockSpec((rows, D_MODEL), lambda i, a, b: (i, 0)),
        scratch_shapes=[pltpu.VMEM((rows, D_MODEL), F32), pltpu.VMEM((rows, D_MODEL), F32),
                        pltpu.SemaphoreType.DMA((2,))],
    )
    return pl.pallas_call(
        functools.partial(_moe_combine_kernel, rows=rows),
        grid_spec=grid_spec,
        out_shape=jax.ShapeDtypeStruct((t, D_MODEL), F32),
        compiler_params=_cparams(("arbitrary",)),
        name="moe_combine",
    )(dest0, dest1, ys, x, route)


def _moe(h, x, route, prm, tm, rows):
    t = h.shape[0]
    n_pairs = t * TOP_K
    n_tiles = n_pairs // tm + N_EXPERTS
    expert = route[:, 0:TOP_K].astype(jnp.int32).reshape(n_pairs)
    onehot = (expert[:, None] == jnp.arange(N_EXPERTS, dtype=jnp.int32)[None, :]).astype(jnp.int32)
    csum = jnp.cumsum(onehot, axis=0)
    rank = jnp.sum(csum * onehot, axis=1) - 1
    counts = csum[-1]
    padded = ((counts + tm - 1) // tm) * tm
    ends = jnp.cumsum(padded)
    starts = ends - padded
    dest = jnp.sum(onehot * starts[None, :], axis=1) + rank
    token = jnp.arange(n_pairs, dtype=jnp.int32) // TOP_K
    src = jnp.zeros((n_tiles * tm,), jnp.int32).at[dest].set(token)
    tile_start = jnp.arange(n_tiles, dtype=jnp.int32) * tm
    tile_expert = jnp.minimum(
        jnp.sum((tile_start[:, None] >= ends[None, :]).astype(jnp.int32), axis=1), N_EXPERTS - 1)
    tile_valid = (tile_start < ends[-1]).astype(jnp.int32)
    ys = _moe_ffn(h, src, tile_expert, tile_valid, prm, tm)
    dest2 = dest.reshape(t, TOP_K)
    return _moe_combine(dest2[:, 0], dest2[:, 1], ys, x, route, rows)


def _rope_tables(pos):
    half = HEAD_DIM // 2
    inv = ROPE_THETA ** (-jnp.arange(half, dtype=F32) / half)
    ang = pos.astype(F32)[:, None] * inv[None, :]
    cos = jnp.cos(ang)
    sin = jnp.sin(ang)
    cos_h = jnp.concatenate([cos, cos], axis=1)
    sin_h = jnp.concatenate([-sin, sin], axis=1)
    return jnp.tile(cos_h, (1, LANES // HEAD_DIM)), jnp.tile(sin_h, (1, LANES // HEAD_DIM))


def _rope_tables_t(pos):
    half = HEAD_DIM // 2
    inv = ROPE_THETA ** (-jnp.arange(half, dtype=F32) / half)
    ang = pos.astype(F32)[:, None] * inv[None, :]
    return jnp.cos(ang).T, jnp.sin(ang).T


def _layer_params(l, p):
    idx = jnp.arange(512)
    tril = jnp.tril(jnp.ones((CHUNK, CHUNK), F32))
    ws = p["gmlp_ws"][l] * tril[None]
    pool_bd = jnp.zeros((P_WIDTH, P_WIDTH), F32)
    for g in range(len(POOL_WINDOWS)):
        pool_bd = pool_bd.at[g * P_GC:(g + 1) * P_GC, g * P_GC:(g + 1) * P_GC].set(p["pool_w"][l, g])
    prm = {
        "g_mix": p["norm_mix_g"][l][None, :],
        "w_in": p["w_in"][l].astype(BF16),
        "w_qkv_t": p["w_in"][l][:, 0:3 * 512].T.astype(BF16),
        "w_rest": p["w_in"][l][:, 1024:].astype(BF16),
        "qg_col": jnp.tile(p["q_norm_g"][l], A_QK_HEADS)[:, None],
        "kg_col": jnp.tile(p["k_norm_g"][l], A_QK_HEADS)[:, None],
        "subg_col": p["subln_g"][l][:, None],
        "head_ones": ((idx[:, None] // HEAD_DIM) == (idx[None, :] // HEAD_DIM)).astype(BF16),
        "qg": jnp.tile(p["q_norm_g"][l], A_QK_HEADS)[None, :],
        "kg": jnp.tile(p["k_norm_g"][l], A_QK_HEADS)[None, :],
        "gn_g": p["gmlp_norm_g"][l][None, :],
        "gn_b": p["gmlp_norm_b"][l][None, :],
        "gw_cat": jnp.concatenate([ws[g] for g in range(G_HEADS)], axis=1).astype(BF16),
        "gb_full": jnp.repeat(p["gmlp_bs"][l].T, HEAD_DIM, axis=1),
        "gw0": jnp.repeat(p["gmlp_ws"][l][:, 0, 0], HEAD_DIM)[None, :],
        "gb0": jnp.repeat(p["gmlp_bs"][l][:, 0], HEAD_DIM)[None, :],
        "pool_bd": pool_bd.astype(BF16),
        "pool_scale": p["pool_scale"][l][None, :],
        "lq1": p["lam_q1"][l][None, :], "lk1": p["lam_k1"][l][None, :],
        "lq2": p["lam_q2"][l][None, :], "lk2": p["lam_k2"][l][None, :],
        "subg": p["subln_g"][l][None, :],
        "wo_a": p["w_out"][l][:A_WIDTH].astype(BF16),
        "wo_gp": p["w_out"][l][A_WIDTH:].astype(BF16),
        "g_ffn": p["norm_ffn_g"][l][None, :],
    }
    j = l // 2
    if l % 2 == 0:
        prm["ffn_wg"] = p["ffn_w_gate"][j].astype(BF16)
        prm["ffn_wu"] = p["ffn_w_up"][j].astype(BF16)
        prm["ffn_wd"] = p["ffn_w_down"][j].astype(BF16)
    else:
        prm["router"] = jnp.pad(p["router_w"][j], ((0, 0), (0, LANES - N_EXPERTS))).astype(BF16)
        prm["moe_wg"] = p["moe_w_gate"][j].astype(BF16)
        prm["moe_wu"] = p["moe_w_up"][j].astype(BF16)
        prm["moe_wd"] = p["moe_w_down"][j].astype(BF16)
    return prm


def kernel(x_prompt, x_sample, cache_k, cache_v, state_pool, page_table, norm_mix_g, w_in, q_norm_g, k_norm_g, lam_q1, lam_k1, lam_q2, lam_k2, subln_g, gmlp_norm_g, gmlp_norm_b, gmlp_ws, gmlp_bs, pool_w, pool_scale, w_out, norm_ffn_g, ffn_w_gate, ffn_w_up, ffn_w_down, router_w, moe_w_gate, moe_w_up, moe_w_down):
    params = dict(norm_mix_g=norm_mix_g, w_in=w_in, q_norm_g=q_norm_g, k_norm_g=k_norm_g,
                  lam_q1=lam_q1, lam_k1=lam_k1, lam_q2=lam_q2, lam_k2=lam_k2, subln_g=subln_g,
                  gmlp_norm_g=gmlp_norm_g, gmlp_norm_b=gmlp_norm_b, gmlp_ws=gmlp_ws,
                  gmlp_bs=gmlp_bs, pool_w=pool_w, pool_scale=pool_scale, w_out=w_out,
                  norm_ffn_g=norm_ffn_g, ffn_w_gate=ffn_w_gate, ffn_w_up=ffn_w_up,
                  ffn_w_down=ffn_w_down, router_w=router_w, moe_w_gate=moe_w_gate,
                  moe_w_up=moe_w_up, moe_w_down=moe_w_down)
    batch, seq_len, _ = x_prompt.shape
    n_dec, dec_seq, _ = x_sample.shape
    assert dec_seq == 1
    depth, n_pool = cache_k.shape[0], cache_k.shape[1]
    past_len = page_table.shape[1] * PAGE_SIZE
    t = batch * seq_len

    cos_pt, sin_pt = _rope_tables_t(jnp.arange(seq_len))
    cos_s, sin_s = _rope_tables(jnp.full((n_dec,), past_len, jnp.int32))
    cache_kt = cache_k.transpose(0, 1, 3, 4, 2).reshape(depth, n_pool, 512, PAGE_SIZE)
    cache_vr = cache_v.reshape(depth, n_pool, PAGE_SIZE * A_HEADS, LANES)

    xp = x_prompt.reshape(t, D_MODEL)
    xs = x_sample.reshape(n_dec, D_MODEL)
    kp_l, vp_l, pp_l, ks_l, vs_l, ps_l, gs_l = [], [], [], [], [], [], []
    for l in range(depth):
        prm = _layer_params(l, params)
        routed = l % 2 == 1

        qt, kt, vt, v, gp, px_last = _mix_in_prompt(xp, prm, cos_pt, sin_pt, seq_len)
        a = _attn_prompt(qt, kt, vt, prm, l)
        if routed:
            xp1, hp, route_p = _mix_out(a, gp, xp, prm, TM_PROMPT, True)
        else:
            xp1, hp = _mix_out(a, gp, xp, prm, TM_PROMPT, False)
        kp_l.append(kt.reshape(batch, A_QK_HEADS, HEAD_DIM, seq_len).transpose(0, 3, 1, 2))
        vp_l.append(v.reshape(batch, seq_len, A_HEADS, 2 * HEAD_DIM))
        pp_l.append(px_last.reshape(batch, TM_PROMPT, P_WIDTH)[:, TM_PROMPT - POOL_HIST:, :])

        hist = state_pool[l].transpose(1, 0, 2)
        q_s, k_s, v_s, gp_s, px_s, gvn_s = _mix_in_sample(xs, prm, cos_s, sin_s, hist)
        a_s = _attn_decode(q_s, k_s, v_s, page_table, cache_kt, cache_vr, prm, l)
        if routed:
            xs1, hs, route_s = _mix_out(a_s, gp_s, xs, prm, n_dec, True)
        else:
            xs1, hs = _mix_out(a_s, gp_s, xs, prm, n_dec, False)
        ks_l.append(k_s.reshape(n_dec, 1, A_QK_HEADS, HEAD_DIM))
        vs_l.append(v_s.reshape(n_dec, 1, A_HEADS, 2 * HEAD_DIM))
        ps_l.append(jnp.concatenate([state_pool[l][:, 1:, :], px_s[:, None, :]], axis=1))
        gs_l.append(gvn_s.reshape(n_dec, 1, G_HEADS, HEAD_DIM))

        if routed:
            xp = _moe(hp, xp1, route_p, prm, TM_MOE, 256)
            xs = _moe(hs, xs1, route_s, prm, n_dec, n_dec)
        else:
            xp = _ffn(hp, xp1, prm, TM_PROMPT)
            xs = _ffn(hs, xs1, prm, n_dec)

    return (xp.reshape(batch, seq_len, D_MODEL), xs.reshape(n_dec, 1, D_MODEL),
            jnp.stack(kp_l), jnp.stack(vp_l), jnp.stack(pp_l), jnp.stack(ks_l), jnp.stack(vs_l),
            jnp.stack(ps_l), jnp.stack(gs_l))
```

```python
import functools
import math

import jax
import jax.numpy as jnp
from jax import lax
from jax.experimental import pallas as pl
from jax.experimental.pallas import tpu as pltpu

F32 = jnp.float32
BF16 = jnp.bfloat16

D_MODEL = 1024
HEAD_DIM = 64
A_HEADS = 4
A_QK_HEADS = 8
A_WIDTH = 512
G_WIDTH = 256
G_HEADS = 4
CHUNK = 128
P_WIDTH = 256
POOL_WINDOWS = (2, 4, 8, 16)
P_GC = 64
POOL_HIST = 15
D_IN = 2304
ROPE_THETA = 10000.0
N_EXPERTS = 8
TOP_K = 2
EPS = 1e-6
PAGE_SIZE = 128

LANES = 128
VMEM_LIMIT = 56 * 1024 * 1024

TM_PROMPT = 512
TQ = 1024
TM_MOE = 512
TF_MOE = 1792
PAGES_PER_STEP = 8
DECODE_SLOTS = 8
DMA_UNROLL = 8


def _cparams(sem):
    return pltpu.CompilerParams(dimension_semantics=sem, vmem_limit_bytes=VMEM_LIMIT)


def _lam_init(layer):
    return 0.8 - 0.6 * math.exp(-0.3 * layer)


def _lam_value(lq1, lk1, lq2, lk2, lam_init):
    a = jnp.sum(lq1 * lk1, axis=-1, keepdims=True)
    b = jnp.sum(lq2 * lk2, axis=-1, keepdims=True)
    return jnp.exp(a) - jnp.exp(b) + lam_init


def _swap_halves(x):
    lane = lax.broadcasted_iota(jnp.int32, x.shape, 1)
    first_half = (lane % HEAD_DIM) < (HEAD_DIM // 2)
    return jnp.where(first_half, pltpu.roll(x, LANES - HEAD_DIM // 2, 1),
                     pltpu.roll(x, HEAD_DIM // 2, 1))


def _head_norm_rope(t, head_ones, gain, cos, sin):
    sq = t * t
    ss = jnp.dot(sq.astype(BF16), head_ones, preferred_element_type=F32)
    ss = ss + jnp.dot((sq - sq.astype(BF16).astype(F32)).astype(BF16), head_ones,
                      preferred_element_type=F32)
    tn = t * lax.rsqrt(ss * (1.0 / HEAD_DIM) + EPS) * gain
    cols = []
    for c in range(t.shape[1] // LANES):
        tc = tn[:, c * LANES:(c + 1) * LANES]
        cols.append(tc * cos + _swap_halves(tc) * sin)
    return jnp.concatenate(cols, axis=1)


def _project(x, g_mix, w_in, head_ones, qg, kg, cos, sin, gn_g, gn_b):
    ms = jnp.mean(x * x, axis=-1, keepdims=True)
    h = x * lax.rsqrt(ms + EPS) * g_mix
    z = jnp.dot(h.astype(BF16), w_in, preferred_element_type=F32)
    q = _head_norm_rope(z[:, 0:512], head_ones, qg, cos, sin)
    k = _head_norm_rope(z[:, 512:1024], head_ones, kg, cos, sin)
    v = z[:, 1024:1536]
    gu = z[:, 1536:1792]
    gv = z[:, 1792:2048]
    px = z[:, 2048:2304]
    mu = jnp.mean(gv, axis=-1, keepdims=True)
    gc = gv - mu
    var = jnp.mean(gc * gc, axis=-1, keepdims=True)
    gvn = gc * lax.rsqrt(var + EPS) * gn_g + gn_b
    return q, k, v, gu, gvn, px


def _pool_window_select(s2, s4, s8, s16):
    lane = lax.broadcasted_iota(jnp.int32, s2.shape, 1)
    return jnp.where(lane < P_GC, s2,
                     jnp.where(lane < 2 * P_GC, s4, jnp.where(lane < 3 * P_GC, s8, s16)))


def _pool_window_sizes(shape):
    lane = lax.broadcasted_iota(jnp.int32, shape, 1)
    return jnp.where(lane < P_GC, 2.0,
                     jnp.where(lane < 2 * P_GC, 4.0,
                               jnp.where(lane < 3 * P_GC, 8.0, 16.0))).astype(F32)


def _head_norm_rope_t(t, gain, cos, sin):
    n = t.shape[1]
    half = HEAD_DIM // 2
    th = t.reshape(A_QK_HEADS, HEAD_DIM, n)
    ms = jnp.mean(th * th, axis=1, keepdims=True)
    tn = (th * lax.rsqrt(ms + EPS)).reshape(A_QK_HEADS * HEAD_DIM, n) * gain
    tn = tn.reshape(A_QK_HEADS, HEAD_DIM, n)
    x1 = tn[:, 0:half, :]
    x2 = tn[:, half:HEAD_DIM, :]
    out = jnp.concatenate([x1 * cos[None] - x2 * sin[None], x2 * cos[None] + x1 * sin[None]],
                          axis=1)
    return out.reshape(A_QK_HEADS * HEAD_DIM, n)


def _mix_in_prompt_kernel(x_ref, gmix_ref, wqkvt_ref, wrest_ref, qg_ref, kg_ref, cos_ref, sin_ref,
                          gng_ref, gnb_ref, gw_ref, gb_ref, pw_ref, ps_ref,
                          qt_ref, kt_ref, vt_ref, v_ref, gp_ref, pxl_ref, ext_ref,
                          *, tm, tiles_per_seq):
    i = pl.program_id(0)
    x = x_ref[...]
    ms = jnp.mean(x * x, axis=-1, keepdims=True)
    h = (x * lax.rsqrt(ms + EPS) * gmix_ref[...]).astype(BF16)
    zt = lax.dot_general(wqkvt_ref[...], h, (((1,), (1,)), ((), ())), preferred_element_type=F32)
    z = jnp.dot(h, wrest_ref[...], preferred_element_type=F32)
    cos = cos_ref[...]
    sin = sin_ref[...]
    qt = _head_norm_rope_t(zt[0:512, :], qg_ref[...], cos, sin)
    qt_ref[0] = (qt * (HEAD_DIM ** -0.5)).astype(BF16)
    kt_ref[0] = _head_norm_rope_t(zt[512:1024, :], kg_ref[...], cos, sin)
    vt_ref[0] = zt[1024:1536, :].astype(BF16)
    v_ref[...] = z[:, 0:512]
    gu = z[:, 512:768]
    gv = z[:, 768:1024]
    px = z[:, 1024:1280]
    mu = jnp.mean(gv, axis=-1, keepdims=True)
    gc = gv - mu
    var = jnp.mean(gc * gc, axis=-1, keepdims=True)
    gvn = gc * lax.rsqrt(var + EPS) * gng_ref[...] + gnb_ref[...]
    pxl_ref[...] = px

    lane = lax.broadcasted_iota(jnp.int32, (CHUNK, G_WIDTH), 1)
    for c in range(tm // CHUNK):
        vn_c = gvn[c * CHUNK:(c + 1) * CHUNK, :]
        rhs = jnp.concatenate(
            [jnp.where((lane // HEAD_DIM) == g, vn_c, 0.0).astype(BF16) for g in range(G_HEADS)],
            axis=0)
        mixed = jnp.dot(gw_ref[...], rhs, preferred_element_type=F32) + gb_ref[...]
        gp_ref[c * CHUNK:(c + 1) * CHUNK, 0:G_WIDTH] = (
            gu[c * CHUNK:(c + 1) * CHUNK, :] * mixed).astype(BF16)

    seq_tile = i % tiles_per_seq

    @pl.when(seq_tile == 0)
    def _():
        ext_ref[0:16, :] = jnp.zeros((16, P_WIDTH), F32)

    @pl.when(seq_tile != 0)
    def _():
        ext_ref[0:16, :] = ext_ref[tm:tm + 16, :]

    ext_ref[16:16 + tm, :] = px

    def back(kk):
        return ext_ref[16 - kk:16 - kk + tm, :]

    s2 = px + back(1)
    s4 = s2 + back(2) + back(3)
    s8 = s4 + back(4) + back(5) + back(6) + back(7)
    s16 = s8
    for kk in range(8, 16):
        s16 = s16 + back(kk)
    tot = _pool_window_select(s2, s4, s8, s16)
    pos = (seq_tile * tm + lax.broadcasted_iota(jnp.int32, (tm, P_WIDTH), 0)).astype(F32)
    cnt = jnp.minimum(_pool_window_sizes((tm, P_WIDTH)), pos + 1.0)
    d = tot / cnt - px
    p = jnp.dot(d.astype(BF16), pw_ref[...], preferred_element_type=F32) * ps_ref[...]
    gp_ref[:, G_WIDTH:G_WIDTH + P_WIDTH] = p.astype(BF16)


def _mix_in_sample_kernel(x_ref, gmix_ref, win_ref, ones_ref, qg_ref, kg_ref, cos_ref, sin_ref,
                          gng_ref, gnb_ref, gw0_ref, gb0_ref, pw_ref, ps_ref, hist_ref,
                          q_ref, k_ref, v_ref, gp_ref, px_ref, gvn_ref):
    q, k, v, gu, gvn, px = _project(
        x_ref[...], gmix_ref[...], win_ref[...], ones_ref[...], qg_ref[...], kg_ref[...],
        cos_ref[...], sin_ref[...], gng_ref[...], gnb_ref[...])
    q_ref[...] = q * (HEAD_DIM ** -0.5)
    k_ref[...] = k
    v_ref[...] = v
    px_ref[...] = px
    gvn_ref[...] = gvn
    mixed = gw0_ref[...] * gvn + gb0_ref[...]
    gp_ref[:, 0:G_WIDTH] = (gu * mixed).astype(BF16)

    def tail(n):
        acc = px
        for r in range(POOL_HIST - n, POOL_HIST):
            acc = acc + hist_ref[r]
        return acc

    tot = _pool_window_select(tail(1), tail(3), tail(7), tail(15))
    d = tot / _pool_window_sizes(px.shape) - px
    p = jnp.dot(d.astype(BF16), pw_ref[...], preferred_element_type=F32) * ps_ref[...]
    gp_ref[:, G_WIDTH:G_WIDTH + P_WIDTH] = p.astype(BF16)


def _const_spec(shape):
    return pl.BlockSpec(shape, lambda *_: (0,) * len(shape))


def _mix_in_prompt(x, prm, cos, sin, seq_len):
    t = x.shape[0]
    tm = TM_PROMPT
    tps = seq_len // tm
    nseq = t // seq_len
    kern = functools.partial(_mix_in_prompt_kernel, tm=tm, tiles_per_seq=tps)
    row = lambda w: pl.BlockSpec((tm, w), lambda i: (i, 0))
    seq_t = pl.BlockSpec((1, 512, tm), lambda i: (i // tps, 0, i % tps))
    return pl.pallas_call(
        kern,
        grid=(t // tm,),
        in_specs=[
            row(D_MODEL), _const_spec((1, D_MODEL)), _const_spec((3 * 512, D_MODEL)),
            _const_spec((D_MODEL, D_IN - 1024)), _const_spec((512, 1)), _const_spec((512, 1)),
            pl.BlockSpec((HEAD_DIM // 2, tm), lambda i: (0, i % tps)),
            pl.BlockSpec((HEAD_DIM // 2, tm), lambda i: (0, i % tps)),
            _const_spec((1, G_WIDTH)), _const_spec((1, G_WIDTH)),
            _const_spec((CHUNK, G_HEADS * CHUNK)), _const_spec((CHUNK, G_WIDTH)),
            _const_spec((P_WIDTH, P_WIDTH)), _const_spec((1, P_WIDTH)),
        ],
        out_specs=[
            seq_t, seq_t, seq_t, row(512), row(512),
            pl.BlockSpec((tm, P_WIDTH), lambda i: (i // tps, 0)),
        ],
        out_shape=[
            jax.ShapeDtypeStruct((nseq, 512, seq_len), BF16),
            jax.ShapeDtypeStruct((nseq, 512, seq_len), F32),
            jax.ShapeDtypeStruct((nseq, 512, seq_len), BF16),
            jax.ShapeDtypeStruct((t, 512), F32),
            jax.ShapeDtypeStruct((t, 512), BF16),
            jax.ShapeDtypeStruct((nseq * tm, P_WIDTH), F32),
        ],
        scratch_shapes=[pltpu.VMEM((tm + 16, P_WIDTH), F32)],
        compiler_params=_cparams(("arbitrary",)),
        name="mix_in_prompt",
    )(x, prm["g_mix"], prm["w_qkv_t"], prm["w_rest"], prm["qg_col"], prm["kg_col"], cos, sin,
      prm["gn_g"], prm["gn_b"], prm["gw_cat"], prm["gb_full"], prm["pool_bd"], prm["pool_scale"])


def _mix_in_sample(x, prm, cos, sin, hist):
    n = x.shape[0]
    full = lambda *s: _const_spec(s)
    return pl.pallas_call(
        _mix_in_sample_kernel,
        grid=(1,),
        in_specs=[
            full(n, D_MODEL), full(1, D_MODEL), full(D_MODEL, D_IN), full(512, 512),
            full(1, 512), full(1, 512), full(n, LANES), full(n, LANES),
            full(1, G_WIDTH), full(1, G_WIDTH), full(1, G_WIDTH), full(1, G_WIDTH),
            full(P_WIDTH, P_WIDTH), full(1, P_WIDTH), full(POOL_HIST, n, P_WIDTH),
        ],
        out_specs=[full(n, 512), full(n, 512), full(n, 512), full(n, 512),
                   full(n, P_WIDTH), full(n, G_WIDTH)],
        out_shape=[
            jax.ShapeDtypeStruct((n, 512), F32),
            jax.ShapeDtypeStruct((n, 512), F32),
            jax.ShapeDtypeStruct((n, 512), F32),
            jax.ShapeDtypeStruct((n, 512), BF16),
            jax.ShapeDtypeStruct((n, P_WIDTH), F32),
            jax.ShapeDtypeStruct((n, G_WIDTH), F32),
        ],
        compiler_params=_cparams(("arbitrary",)),
        name="mix_in_sample",
    )(x, prm["g_mix"], prm["w_in"], prm["head_ones"], prm["qg"], prm["kg"], cos, sin,
      prm["gn_g"], prm["gn_b"], prm["gw0"], prm["gb0"], prm["pool_bd"], prm["pool_scale"], hist)


def _subln(o, subg, lam_init):
    ms = jnp.mean(o * o, axis=-1, keepdims=True)
    return o * lax.rsqrt(ms + EPS) * subg * (1.0 - lam_init)


def _attn_prompt_kernel(qt_ref, kt_ref, vt_ref, lq1_ref, lk1_ref, lq2_ref, lk2_ref, subg_ref,
                        o_ref, qs_ref, m_ref, l_ref, acc_ref, *, tq, lam_init):
    qi = pl.program_id(2)
    q = qt_ref[0]
    row = lax.broadcasted_iota(jnp.int32, q.shape, 0)
    zero = jnp.zeros_like(q)
    qs_ref[:, 0:tq] = jnp.where(row < HEAD_DIM, q, zero)
    qs_ref[:, tq:2 * tq] = jnp.where(row >= HEAD_DIM, q, zero)
    m_ref[...] = jnp.full(m_ref.shape, -jnp.inf, F32)
    l_ref[...] = jnp.zeros(l_ref.shape, F32)
    acc_ref[...] = jnp.zeros(acc_ref.shape, F32)

    def update(ki, masked):
        keys = pl.ds(pl.multiple_of(ki * tq, tq), tq)
        s = lax.dot_general(kt_ref[0, :, keys].astype(BF16), qs_ref[...],
                            (((0,), (0,)), ((), ())), preferred_element_type=F32)
        if masked:
            r = lax.broadcasted_iota(jnp.int32, s.shape, 0)
            c = lax.broadcasted_iota(jnp.int32, s.shape, 1) % tq
            s = jnp.where(r <= c, s, -jnp.inf)
        m_prev = m_ref[...]
        m_new = jnp.maximum(m_prev, jnp.max(s, axis=0, keepdims=True))
        alpha = jnp.exp(m_prev - m_new)
        p = jnp.exp(s - m_new[0:1, :])
        l_ref[...] = alpha * l_ref[...] + jnp.sum(p, axis=0, keepdims=True)
        acc_ref[...] = alpha[0:1, :] * acc_ref[...] + jnp.dot(
            vt_ref[0, :, keys], p.astype(BF16), preferred_element_type=F32)
        m_ref[...] = m_new

    def below_diagonal(ki, carry):
        update(ki, False)
        return carry

    lax.fori_loop(0, qi, below_diagonal, 0)
    update(qi, True)
    o = acc_ref[...] / l_ref[0:1, :]
    lam = _lam_value(lq1_ref[...], lk1_ref[...], lq2_ref[...], lk2_ref[...], lam_init)
    od = o[:, 0:tq] - lam * o[:, tq:2 * tq]
    ms = jnp.mean(od * od, axis=0, keepdims=True)
    o_ref[0] = (od * lax.rsqrt(ms + EPS) * subg_ref[...] * (1.0 - lam_init)).astype(BF16)


def _attn_prompt(qt, kt, vt, prm, layer):
    nseq, _, seq_len = qt.shape
    tq = TQ
    kern = functools.partial(_attn_prompt_kernel, tq=tq, lam_init=_lam_init(layer))
    vec = lambda w: pl.BlockSpec((1, w), lambda b, h, i: (0, 0))
    q_tile = pl.BlockSpec((1, LANES, tq), lambda b, h, i: (b, h, i))
    head_resident = pl.BlockSpec((1, LANES, seq_len), lambda b, h, i: (b, h, 0))
    return pl.pallas_call(
        kern,
        grid=(nseq, A_HEADS, seq_len // tq),
        in_specs=[
            q_tile, head_resident, head_resident,
            vec(HEAD_DIM), vec(HEAD_DIM), vec(HEAD_DIM), vec(HEAD_DIM),
            pl.BlockSpec((LANES, 1), lambda b, h, i: (0, 0)),
        ],
        out_specs=q_tile,
        out_shape=jax.ShapeDtypeStruct((nseq, 512, seq_len), BF16),
        scratch_shapes=[
            pltpu.VMEM((LANES, 2 * tq), BF16),
            pltpu.VMEM((8, 2 * tq), F32),
            pltpu.VMEM((8, 2 * tq), F32),
            pltpu.VMEM((LANES, 2 * tq), F32),
        ],
        compiler_params=_cparams(("arbitrary", "arbitrary", "arbitrary")),
        name="attn_prompt",
    )(qt, kt, vt, prm["lq1"], prm["lk1"], prm["lq2"], prm["lk2"], prm["subg_col"])


def _head_scores(qb, kt):
    prod = qb * kt
    return jnp.sum(prod.reshape(A_QK_HEADS, HEAD_DIM, LANES), axis=1)


def _attn_decode_kernel(pt_ref, qb_ref, knb_ref, vn_ref, lq1_ref, lk1_ref, lq2_ref, lk2_ref,
                        subg_ref, ck_ref, cv_ref, o_ref, buf_ref, sc_ref, sem,
                        *, npg, layer, lam_init):
    b = pl.program_id(0)
    n_seq = pl.num_programs(0)
    n_groups = pt_ref.shape[1] // npg

    def bf16_round(x):
        return x.astype(BF16).astype(F32)

    qb = bf16_round(qb_ref[0])

    def start_group(cache_ref, seq, grp, slot):
        for i in range(npg):
            page = pt_ref[seq, grp * npg + i]
            pltpu.make_async_copy(cache_ref.at[layer, page], buf_ref.at[slot, i],
                                  sem.at[slot]).start()

    def wait_group(slot):
        pltpu.make_async_copy(ck_ref.at[layer, pl.ds(0, npg)], buf_ref.at[slot],
                              sem.at[slot]).wait()

    n_items = 2 * n_groups
    lookahead = DECODE_SLOTS - 1

    def start_item(seq, item):
        slot = item % DECODE_SLOTS

        @pl.when(item < n_groups)
        def _():
            start_group(ck_ref, seq, item, slot)

        @pl.when(item >= n_groups)
        def _():
            start_group(cv_ref, seq, item - n_groups, slot)

    def prefetch(item):
        nxt = item + lookahead

        @pl.when(nxt < n_items)
        def _():
            start_item(b, nxt)

        @pl.when((nxt >= n_items) & (b + 1 < n_seq))
        def _():
            start_item(jnp.minimum(b + 1, n_seq - 1), nxt - n_items)

    @pl.when(b == 0)
    def _():
        for item in range(lookahead):
            start_group(ck_ref, 0, item, item % DECODE_SLOTS)

    def key_pass(grp, m_run):
        slot = grp % DECODE_SLOTS
        prefetch(grp)
        wait_group(slot)
        sc = jnp.concatenate(
            [_head_scores(qb, bf16_round(buf_ref[slot, i])) for i in range(npg)], axis=1)
        sc_ref[grp] = sc
        return jnp.maximum(m_run, jnp.max(sc, axis=1, keepdims=True))

    s_new = _head_scores(qb_ref[0], knb_ref[0])
    m_fin = lax.fori_loop(0, n_groups, key_pass, s_new)

    def exp_pass(grp, l_run):
        p = jnp.exp(sc_ref[grp] - m_fin[:, 0:1])
        sc_ref[grp] = p
        return l_run + jnp.sum(p, axis=1, keepdims=True)

    p_new = jnp.exp(s_new - m_fin)
    l_fin = lax.fori_loop(0, n_groups, exp_pass, jnp.zeros((A_QK_HEADS, LANES), F32)) + p_new

    lam = _lam_value(lq1_ref[...], lk1_ref[...], lq2_ref[...], lk2_ref[...], lam_init)
    row8 = lax.broadcasted_iota(jnp.int32, (A_QK_HEADS, LANES), 0)
    coef = jnp.where(row8 % 2 == 0, 1.0, -lam)

    def diff_weights(p):
        n = p.shape[1] // LANES
        wc = p / jnp.tile(l_fin, (1, n)) * jnp.tile(coef, (1, n))
        wd = wc + pltpu.roll(wc, A_QK_HEADS - 1, 0)
        even = lax.broadcasted_iota(jnp.int32, wd.shape, 0) % 2 == 0
        return jnp.where(even, wd, 0.0)

    head_of_row = lax.broadcasted_iota(jnp.int32, (A_QK_HEADS, A_HEADS * LANES), 0) // 2
    head_of_col = lax.broadcasted_iota(jnp.int32, (A_QK_HEADS, A_HEADS * LANES), 1) // LANES

    def value_pass(grp, acc):
        slot = (n_groups + grp) % DECODE_SLOTS
        prefetch(n_groups + grp)
        wait_group(slot)
        w = diff_weights(sc_ref[grp])
        p_parts, v_parts = [], []
        for i in range(npg):
            wi = w[:, i * LANES:(i + 1) * LANES]
            p_parts.append(jnp.where(head_of_row == head_of_col,
                                     jnp.concatenate([wi] * A_HEADS, axis=1), 0.0).astype(BF16))
            for h in range(A_HEADS):
                v_parts.append(
                    buf_ref[slot, i, pl.ds(h, PAGE_SIZE, stride=A_HEADS), :].astype(BF16))
        return acc + jnp.dot(jnp.concatenate(p_parts, axis=1), jnp.concatenate(v_parts, axis=0),
                             preferred_element_type=F32)

    acc = lax.fori_loop(0, n_groups, value_pass, jnp.zeros((A_QK_HEADS, LANES), F32))
    acc = acc + diff_weights(p_new) * vn_ref[0]
    o_ref[0] = _subln(acc, subg_ref[...], lam_init)


def _attn_decode(q_s, k_s, v_s, page_table, cache_kt, cache_vr, prm, layer):
    n = q_s.shape[0]
    npg = PAGES_PER_STEP
    assert page_table.shape[1] % npg == 0
    n_groups = page_table.shape[1] // npg
    assert (2 * n_groups) % DECODE_SLOTS == 0 and DECODE_SLOTS - 1 <= n_groups
    qb = jnp.broadcast_to(q_s[:, :, None], (n, 512, LANES))
    knb = jnp.broadcast_to(k_s[:, :, None], (n, 512, LANES))
    vn8 = jnp.repeat(v_s.reshape(n, A_HEADS, LANES), 2, axis=1)
    kern = functools.partial(_attn_decode_kernel, npg=npg, layer=layer,
                             lam_init=_lam_init(layer))
    vec = lambda w: pl.BlockSpec((1, w), lambda b, pt: (0, 0))
    seq3 = lambda r: pl.BlockSpec((1, r, LANES), lambda b, pt: (b, 0, 0))
    grid_spec = pltpu.PrefetchScalarGridSpec(
        num_scalar_prefetch=1,
        grid=(n,),
        in_specs=[seq3(512), seq3(512), seq3(A_QK_HEADS),
                  vec(HEAD_DIM), vec(HEAD_DIM), vec(HEAD_DIM), vec(HEAD_DIM), vec(LANES),
                  pl.BlockSpec(memory_space=pl.ANY), pl.BlockSpec(memory_space=pl.ANY)],
        out_specs=seq3(A_QK_HEADS),
        scratch_shapes=[pltpu.VMEM((DECODE_SLOTS, npg, PAGE_SIZE * A_HEADS, LANES), F32),
                        pltpu.VMEM((n_groups, A_QK_HEADS, npg * PAGE_SIZE), F32),
                        pltpu.SemaphoreType.DMA((DECODE_SLOTS,))],
    )
    out = pl.pallas_call(
        kern,
        grid_spec=grid_spec,
        out_shape=jax.ShapeDtypeStruct((n, A_QK_HEADS, LANES), F32),
        compiler_params=_cparams(("arbitrary",)),
        name="attn_decode",
    )(page_table, qb, knb, vn8, prm["lq1"], prm["lk1"], prm["lq2"], prm["lk2"], prm["subg"],
      cache_kt, cache_vr)
    return out[:, 0::2, :].reshape(n, A_WIDTH).astype(BF16)


def _mix_out_kernel(a_ref, gp_ref, x_ref, woa_ref, wogp_ref, g_ref, *rest, routed, a_transposed):
    if routed:
        rw_ref, x1_ref, h_ref, route_ref = rest
    else:
        x1_ref, h_ref = rest
    if a_transposed:
        y = lax.dot_general(a_ref[0], woa_ref[...], (((0,), (0,)), ((), ())),
                            preferred_element_type=F32)
    else:
        y = jnp.dot(a_ref[...], woa_ref[...], preferred_element_type=F32)
    y = y + jnp.dot(gp_ref[...], wogp_ref[...], preferred_element_type=F32)
    x1 = x_ref[...] + y
    x1_ref[...] = x1
    ms = jnp.mean(x1 * x1, axis=-1, keepdims=True)
    h = x1 * lax.rsqrt(ms + EPS) * g_ref[...]
    h_ref[...] = h.astype(h_ref.dtype)
    if routed:
        logits = jnp.dot(h.astype(BF16), rw_ref[...], preferred_element_type=F32)
        lane = lax.broadcasted_iota(jnp.int32, logits.shape, 1)
        lg = jnp.where(lane < N_EXPERTS, logits, -jnp.inf)
        m1 = jnp.max(lg, axis=1, keepdims=True)
        i1 = jnp.min(jnp.where(lg == m1, lane, LANES), axis=1, keepdims=True)
        lg2 = jnp.where(lane == i1, -jnp.inf, lg)
        m2 = jnp.max(lg2, axis=1, keepdims=True)
        i2 = jnp.min(jnp.where(lg2 == m2, lane, LANES), axis=1, keepdims=True)
        e2 = jnp.exp(m2 - m1)
        den = 1.0 + e2
        route = jnp.where(lane == 0, i1.astype(F32),
                          jnp.where(lane == 1, i2.astype(F32),
                                    jnp.where(lane == 2, 1.0 / den,
                                              jnp.where(lane == 3, e2 / den, 0.0))))
        route_ref[...] = route


def _mix_out(a, gp, x, prm, tm, routed):
    t = x.shape[0]
    row = lambda w: pl.BlockSpec((tm, w), lambda i: (i, 0))
    a_transposed = a.ndim == 3
    if a_transposed:
        tps = a.shape[2] // tm
        a_spec = pl.BlockSpec((1, 512, tm), lambda i: (i // tps, 0, i % tps))
    else:
        a_spec = row(512)
    in_specs = [a_spec, row(512), row(D_MODEL), _const_spec((512, D_MODEL)),
                _const_spec((512, D_MODEL)), _const_spec((1, D_MODEL))]
    args = [a, gp, x, prm["wo_a"], prm["wo_gp"], prm["g_ffn"]]
    out_specs = [row(D_MODEL), row(D_MODEL)]
    out_shape = [jax.ShapeDtypeStruct((t, D_MODEL), F32),
                 jax.ShapeDtypeStruct((t, D_MODEL), F32 if routed else BF16)]
    if routed:
        in_specs.append(_const_spec((D_MODEL, LANES)))
        args.append(prm["router"])
        out_specs.append(row(LANES))
        out_shape.append(jax.ShapeDtypeStruct((t, LANES), F32))
    return pl.pallas_call(
        functools.partial(_mix_out_kernel, routed=routed, a_transposed=a_transposed),
        grid=(t // tm,),
        in_specs=in_specs, out_specs=out_specs, out_shape=out_shape,
        compiler_params=_cparams(("arbitrary",)),
        name="mix_out",
    )(*args)


def _silu(x):
    return x * (1.0 / (1.0 + jnp.exp(-x)))


def _ffn_kernel(h_ref, x_ref, wg_ref, wu_ref, wd_ref, o_ref, *, n_chunks):
    h = h_ref[...]
    f = wg_ref.shape[1]
    fc = f // n_chunks
    acc = x_ref[...]
    for c in range(n_chunks):
        gate = jnp.dot(h, wg_ref[:, c * fc:(c + 1) * fc], preferred_element_type=F32)
        up = jnp.dot(h, wu_ref[:, c * fc:(c + 1) * fc], preferred_element_type=F32)
        act = (_silu(gate) * up).astype(BF16)
        acc = acc + jnp.dot(act, wd_ref[c * fc:(c + 1) * fc, :], preferred_element_type=F32)
    o_ref[...] = acc


def _ffn(h, x, prm, tm):
    t = x.shape[0]
    f = prm["ffn_wg"].shape[1]
    row = lambda w: pl.BlockSpec((tm, w), lambda i: (i, 0))
    resident = lambda s: pl.BlockSpec(s, lambda i: (0, 0), pipeline_mode=pl.Buffered(1))
    return pl.pallas_call(
        functools.partial(_ffn_kernel, n_chunks=2),
        grid=(t // tm,),
        in_specs=[row(D_MODEL), row(D_MODEL), resident((D_MODEL, f)), resident((D_MODEL, f)),
                  resident((f, D_MODEL))],
        out_specs=row(D_MODEL),
        out_shape=jax.ShapeDtypeStruct((t, D_MODEL), F32),
        compiler_params=_cparams(("arbitrary",)),
        name="ffn_dense",
    )(h, x, prm["ffn_wg"], prm["ffn_wu"], prm["ffn_wd"])


def _moe_ffn_kernel(te_ref, tv_ref, src_ref, h_ref, wg_ref, wu_ref, wd_ref, o_ref,
                    xg_ref, xb_ref, sem, *, tm):
    i = pl.program_id(0)
    j = pl.program_id(1)
    n_tiles = pl.num_programs(0)
    valid = tv_ref[i] != 0

    def start_gather(tile):
        slot = tile % 2
        base = tile * tm

        def issue(g, carry):
            for u in range(DMA_UNROLL):
                r = g * DMA_UNROLL + u
                pltpu.make_async_copy(h_ref.at[pl.ds(src_ref[base + r], 1)],
                                      xg_ref.at[slot, pl.ds(r, 1)],
                                      sem.at[slot]).start(priority=u % 2)
            return carry

        lax.fori_loop(0, tm // DMA_UNROLL, issue, 0)

    @pl.when((i == 0) & (j == 0) & valid)
    def _():
        start_gather(i)

    @pl.when((j == 0) & valid)
    def _():
        slot = i % 2
        pltpu.make_async_copy(h_ref.at[pl.ds(0, tm)], xg_ref.at[slot], sem.at[slot]).wait()
        xb_ref[...] = xg_ref[slot].astype(BF16)

    @pl.when(j == 0)
    def _():
        o_ref[...] = jnp.zeros(o_ref.shape, F32)

    @pl.when((j == 1) & (i + 1 < n_tiles))
    def _():
        @pl.when(tv_ref[jnp.minimum(i + 1, n_tiles - 1)] != 0)
        def _():
            start_gather(i + 1)

    @pl.when(valid)
    def _():
        xb = xb_ref[...]
        gate = jnp.dot(xb, wg_ref[0], preferred_element_type=F32)
        up = jnp.dot(xb, wu_ref[0], preferred_element_type=F32)
        act = (_silu(gate) * up).astype(BF16)
        o_ref[...] += jnp.dot(act, wd_ref[0], preferred_element_type=F32)


def _moe_ffn(h, src, tile_expert, tile_valid, prm, tm):
    n = src.shape[0]
    f = prm["moe_wg"].shape[2]
    tf = TF_MOE
    assert f // tf >= 2
    grid_spec = pltpu.PrefetchScalarGridSpec(
        num_scalar_prefetch=3,
        grid=(n // tm, f // tf),
        in_specs=[
            pl.BlockSpec(memory_space=pl.ANY),
            pl.BlockSpec((1, D_MODEL, tf), lambda i, j, te, tv, sr: (te[i], 0, j)),
            pl.BlockSpec((1, D_MODEL, tf), lambda i, j, te, tv, sr: (te[i], 0, j)),
            pl.BlockSpec((1, tf, D_MODEL), lambda i, j, te, tv, sr: (te[i], j, 0)),
        ],
        out_specs=pl.BlockSpec((tm, D_MODEL), lambda i, j, te, tv, sr: (i, 0)),
        scratch_shapes=[pltpu.VMEM((2, tm, D_MODEL), F32), pltpu.VMEM((tm, D_MODEL), BF16),
                        pltpu.SemaphoreType.DMA((2,))],
    )
    return pl.pallas_call(
        functools.partial(_moe_ffn_kernel, tm=tm),
        grid_spec=grid_spec,
        out_shape=jax.ShapeDtypeStruct((n, D_MODEL), F32),
        compiler_params=_cparams(("arbitrary", "arbitrary")),
        name="moe_ffn",
    )(tile_expert, tile_valid, src, h, prm["moe_wg"], prm["moe_wu"], prm["moe_wd"])


def _moe_combine_kernel(d0_ref, d1_ref, ys_ref, x_ref, route_ref, o_ref, b0_ref, b1_ref, sem,
                        *, rows):
    base = pl.program_id(0) * rows

    def issue(g, carry):
        for u in range(DMA_UNROLL):
            r = g * DMA_UNROLL + u
            pltpu.make_async_copy(ys_ref.at[pl.ds(d0_ref[base + r], 1)], b0_ref.at[pl.ds(r, 1)],
                                  sem.at[0]).start(priority=0)
            pltpu.make_async_copy(ys_ref.at[pl.ds(d1_ref[base + r], 1)], b1_ref.at[pl.ds(r, 1)],
                                  sem.at[1]).start(priority=1)
        return carry

    lax.fori_loop(0, rows // DMA_UNROLL, issue, 0)
    pltpu.make_async_copy(ys_ref.at[pl.ds(0, rows)], b0_ref, sem.at[0]).wait()
    pltpu.make_async_copy(ys_ref.at[pl.ds(0, rows)], b1_ref, sem.at[1]).wait()
    route = route_ref[...]
    y = route[:, 2:3] * b0_ref[...] + route[:, 3:4] * b1_ref[...]
    o_ref[...] = x_ref[...] + y


def _moe_combine(dest0, dest1, ys, x, route, rows):
    t = x.shape[0]
    grid_spec = pltpu.PrefetchScalarGridSpec(
        num_scalar_prefetch=2,
        grid=(t // rows,),
        in_specs=[
            pl.BlockSpec(memory_space=pl.ANY),
            pl.BlockSpec((rows, D_MODEL), lambda i, a, b: (i, 0)),
            pl.BlockSpec((rows, LANES), lambda i, a, b: (i, 0)),
        ],
        out_specs=pl.BlockSpec((rows, D_MODEL), lambda i, a, b: (i, 0)),
        scratch_shapes=[pltpu.VMEM((rows, D_MODEL), F32), pltpu.VMEM((rows, D_MODEL), F32),
                        pltpu.SemaphoreType.DMA((2,))],
    )
    return pl.pallas_call(
        functools.partial(_moe_combine_kernel, rows=rows),
        grid_spec=grid_spec,
        out_shape=jax.ShapeDtypeStruct((t, D_MODEL), F32),
        compiler_params=_cparams(("arbitrary",)),
        name="moe_combine",
    )(dest0, dest1, ys, x, route)


def _moe(h, x, route, prm, tm, rows):
    t = h.shape[0]
    n_pairs = t * TOP_K
    n_tiles = n_pairs // tm + N_EXPERTS
    expert = route[:, 0:TOP_K].astype(jnp.int32).reshape(n_pairs)
    onehot = (expert[:, None] == jnp.arange(N_EXPERTS, dtype=jnp.int32)[None, :]).astype(jnp.int32)
    csum = jnp.cumsum(onehot, axis=0)
    rank = jnp.sum(csum * onehot, axis=1) - 1
    counts = csum[-1]
    padded = ((counts + tm - 1) // tm) * tm
    ends = jnp.cumsum(padded)
    starts = ends - padded
    dest = jnp.sum(onehot * starts[None, :], axis=1) + rank
    token = jnp.arange(n_pairs, dtype=jnp.int32) // TOP_K
    src = jnp.zeros((n_tiles * tm,), jnp.int32).at[dest].set(token)
    tile_start = jnp.arange(n_tiles, dtype=jnp.int32) * tm
    tile_expert = jnp.minimum(
        jnp.sum((tile_start[:, None] >= ends[None, :]).astype(jnp.int32), axis=1), N_EXPERTS - 1)
    tile_valid = (tile_start < ends[-1]).astype(jnp.int32)
    ys = _moe_ffn(h, src, tile_expert, tile_valid, prm, tm)
    dest2 = dest.reshape(t, TOP_K)
    return _moe_combine(dest2[:, 0], dest2[:, 1], ys, x, route, rows)


def _rope_tables(pos):
    half = HEAD_DIM // 2
    inv = ROPE_THETA ** (-jnp.arange(half, dtype=F32) / half)
    ang = pos.astype(F32)[:, None] * inv[None, :]
    cos = jnp.cos(ang)
    sin = jnp.sin(ang)
    cos_h = jnp.concatenate([cos, cos], axis=1)
    sin_h = jnp.concatenate([-sin, sin], axis=1)
    return jnp.tile(cos_h, (1, LANES // HEAD_DIM)), jnp.tile(sin_h, (1, LANES // HEAD_DIM))


def _rope_tables_t(pos):
    half = HEAD_DIM // 2
    inv = ROPE_THETA ** (-jnp.arange(half, dtype=F32) / half)
    ang = pos.astype(F32)[:, None] * inv[None, :]
    return jnp.cos(ang).T, jnp.sin(ang).T


def _layer_params(l, p):
    idx = jnp.arange(512)
    tril = jnp.tril(jnp.ones((CHUNK, CHUNK), F32))
    ws = p["gmlp_ws"][l] * tril[None]
    pool_bd = jnp.zeros((P_WIDTH, P_WIDTH), F32)
    for g in range(len(POOL_WINDOWS)):
        pool_bd = pool_bd.at[g * P_GC:(g + 1) * P_GC, g * P_GC:(g + 1) * P_GC].set(p["pool_w"][l, g])
    prm = {
        "g_mix": p["norm_mix_g"][l][None, :],
        "w_in": p["w_in"][l].astype(BF16),
        "w_qkv_t": p["w_in"][l][:, 0:3 * 512].T.astype(BF16),
        "w_rest": p["w_in"][l][:, 1024:].astype(BF16),
        "qg_col": jnp.tile(p["q_norm_g"][l], A_QK_HEADS)[:, None],
        "kg_col": jnp.tile(p["k_norm_g"][l], A_QK_HEADS)[:, None],
        "subg_col": p["subln_g"][l][:, None],
        "head_ones": ((idx[:, None] // HEAD_DIM) == (idx[None, :] // HEAD_DIM)).astype(BF16),
        "qg": jnp.tile(p["q_norm_g"][l], A_QK_HEADS)[None, :],
        "kg": jnp.tile(p["k_norm_g"][l], A_QK_HEADS)[None, :],
        "gn_g": p["gmlp_norm_g"][l][None, :],
        "gn_b": p["gmlp_norm_b"][l][None, :],
        "gw_cat": jnp.concatenate([ws[g] for g in range(G_HEADS)], axis=1).astype(BF16),
        "gb_full": jnp.repeat(p["gmlp_bs"][l].T, HEAD_DIM, axis=1),
        "gw0": jnp.repeat(p["gmlp_ws"][l][:, 0, 0], HEAD_DIM)[None, :],
        "gb0": jnp.repeat(p["gmlp_bs"][l][:, 0], HEAD_DIM)[None, :],
        "pool_bd": pool_bd.astype(BF16),
        "pool_scale": p["pool_scale"][l][None, :],
        "lq1": p["lam_q1"][l][None, :], "lk1": p["lam_k1"][l][None, :],
        "lq2": p["lam_q2"][l][None, :], "lk2": p["lam_k2"][l][None, :],
        "subg": p["subln_g"][l][None, :],
        "wo_a": p["w_out"][l][:A_WIDTH].astype(BF16),
        "wo_gp": p["w_out"][l][A_WIDTH:].astype(BF16),
        "g_ffn": p["norm_ffn_g"][l][None, :],
    }
    j = l // 2
    if l % 2 == 0:
        prm["ffn_wg"] = p["ffn_w_gate"][j].astype(BF16)
        prm["ffn_wu"] = p["ffn_w_up"][j].astype(BF16)
        prm["ffn_wd"] = p["ffn_w_down"][j].astype(BF16)
    else:
        prm["router"] = jnp.pad(p["router_w"][j], ((0, 0), (0, LANES - N_EXPERTS))).astype(BF16)
        prm["moe_wg"] = p["moe_w_gate"][j].astype(BF16)
        prm["moe_wu"] = p["moe_w_up"][j].astype(BF16)
        prm["moe_wd"] = p["moe_w_down"][j].astype(BF16)
    return prm


def kernel(x_prompt, x_sample, cache_k, cache_v, state_pool, page_table, norm_mix_g, w_in, q_norm_g, k_norm_g, lam_q1, lam_k1, lam_q2, lam_k2, subln_g, gmlp_norm_g, gmlp_norm_b, gmlp_ws, gmlp_bs, pool_w, pool_scale, w_out, norm_ffn_g, ffn_w_gate, ffn_w_up, ffn_w_down, router_w, moe_w_gate, moe_w_up, moe_w_down):
    params = dict(norm_mix_g=norm_mix_g, w_in=w_in, q_norm_g=q_norm_g, k_norm_g=k_norm_g,
                  lam_q1=lam_q1, lam_k1=lam_k1, lam_q2=lam_q2, lam_k2=lam_k2, subln_g=subln_g,
                  gmlp_norm_g=gmlp_norm_g, gmlp_norm_b=gmlp_norm_b, gmlp_ws=gmlp_ws,
                  gmlp_bs=gmlp_bs, pool_w=pool_w, pool_scale=pool_scale, w_out=w_out,
                  norm_ffn_g=norm_ffn_g, ffn_w_gate=ffn_w_gate, ffn_w_up=ffn_w_up,
                  ffn_w_down=ffn_w_down, router_w=router_w, moe_w_gate=moe_w_gate,
                  moe_w_up=moe_w_up, moe_w_down=moe_w_down)
    batch, seq_len, _ = x_prompt.shape
    n_dec, dec_seq, _ = x_sample.shape
    assert dec_seq == 1
    depth, n_pool = cache_k.shape[0], cache_k.shape[1]
    past_len = page_table.shape[1] * PAGE_SIZE
    t = batch * seq_len

    cos_pt, sin_pt = _rope_tables_t(jnp.arange(seq_len))
    cos_s, sin_s = _rope_tables(jnp.full((n_dec,), past_len, jnp.int32))
    cache_kt = cache_k.transpose(0, 1, 3, 4, 2).reshape(depth, n_pool, 512, PAGE_SIZE)
    cache_vr = cache_v.reshape(depth, n_pool, PAGE_SIZE * A_HEADS, LANES)

    xp = x_prompt.reshape(t, D_MODEL)
    xs = x_sample.reshape(n_dec, D_MODEL)
    kp_l, vp_l, pp_l, ks_l, vs_l, ps_l, gs_l = [], [], [], [], [], [], []
    for l in range(depth):
        prm = _layer_params(l, params)
        routed = l % 2 == 1

        qt, kt, vt, v, gp, px_last = _mix_in_prompt(xp, prm, cos_pt, sin_pt, seq_len)
        a = _attn_prompt(qt, kt, vt, prm, l)
        if routed:
            xp1, hp, route_p = _mix_out(a, gp, xp, prm, TM_PROMPT, True)
        else:
            xp1, hp = _mix_out(a, gp, xp, prm, TM_PROMPT, False)
        kp_l.append(kt.reshape(batch, A_QK_HEADS, HEAD_DIM, seq_len).transpose(0, 3, 1, 2))
        vp_l.append(v.reshape(batch, seq_len, A_HEADS, 2 * HEAD_DIM))
        pp_l.append(px_last.reshape(batch, TM_PROMPT, P_WIDTH)[:, TM_PROMPT - POOL_HIST:, :])

        hist = state_pool[l].transpose(1, 0, 2)
        q_s, k_s, v_s, gp_s, px_s, gvn_s = _mix_in_sample(xs, prm, cos_s, sin_s, hist)
        a_s = _attn_decode(q_s, k_s, v_s, page_table, cache_kt, cache_vr, prm, l)
        if routed:
            xs1, hs, route_s = _mix_out(a_s, gp_s, xs, prm, n_dec, True)
        else:
            xs1, hs = _mix_out(a_s, gp_s, xs, prm, n_dec, False)
        ks_l.append(k_s.reshape(n_dec, 1, A_QK_HEADS, HEAD_DIM))
        vs_l.append(v_s.reshape(n_dec, 1, A_HEADS, 2 * HEAD_DIM))
        ps_l.append(jnp.concatenate([state_pool[l][:, 1:, :], px_s[:, None, :]], axis=1))
        gs_l.append(gvn_s.reshape(n_dec, 1, G_HEADS, HEAD_DIM))

        if routed:
            xp = _moe(hp, xp1, route_p, prm, TM_MOE, 256)
            xs = _moe(hs, xs1, route_s, prm, n_dec, n_dec)
        else:
            xp = _ffn(hp, xp1, prm, TM_PROMPT)
            xs = _ffn(hs, xs1, prm, n_dec)

    return (xp.reshape(batch, seq_len, D_MODEL), xs.reshape(n_dec, 1, D_MODEL),
            jnp.stack(kp_l), jnp.stack(vp_l), jnp.stack(pp_l), jnp.stack(ks_l), jnp.stack(vs_l),
            jnp.stack(ps_l), jnp.stack(gs_l))
```
